```python
import math
import jax, jax.numpy as jnp
from jax import lax
import numpy as np

D_MODEL = 2048
BATCH = 1
SEQ = 8192
DEPTH = 2

N_MIXERS = 2
PLE_DIM = 256
CONV_K = 31
FFN_CONV_K = 3
D_FF = 5632
DIFF_HEAD_DIM = 128
DIFF_HEADS = D_MODEL // (2 * DIFF_HEAD_DIM)
Q_BLOCK = 128
N_CONV_LAYERS = (DEPTH + 1) // 2
N_ATTN_LAYERS = DEPTH // 2
RMS_EPS = 1e-6
LN_EPS = 1e-5
MASK_VALUE = -1e30

kernel_name = "hybrid_conformer_conv_diff_attn_convffn"


def rms_norm(x, g, eps=RMS_EPS):
    xf = x.astype(jnp.float32)
    y = xf * lax.rsqrt(jnp.mean(xf * xf, axis=-1, keepdims=True) + eps)
    return (y * g.astype(jnp.float32)).astype(x.dtype)


def layer_norm(x, g, b, eps=LN_EPS):
    xf = x.astype(jnp.float32)
    mu = jnp.mean(xf, axis=-1, keepdims=True)
    xc = xf - mu
    y = xc * lax.rsqrt(jnp.mean(xc * xc, axis=-1, keepdims=True) + eps)
    return (y * g.astype(jnp.float32) + b.astype(jnp.float32)).astype(x.dtype)


def causal_dwconv(x, w):
    k, c = w.shape
    return lax.conv_general_dilated(
        x, w[:, None, :].astype(x.dtype), window_strides=(1,), padding=[(k - 1, 0)],
        dimension_numbers=("NWC", "WIO", "NWC"), feature_group_count=c)


def lambda_init_fn(layer_idx):
    return 0.8 - 0.6 * math.exp(-0.3 * layer_idx)


def conformer_conv(x, w_pw1, b_pw1, w_dw, b_dw, ln_g, ln_b, w_pw2, b_pw2):
    a, g = jnp.split(x @ w_pw1 + b_pw1, 2, axis=-1)
    u = a * jax.nn.sigmoid(g)
    u = causal_dwconv(u, w_dw) + b_dw
    u = jax.nn.silu(layer_norm(u, ln_g, ln_b))
    return u @ w_pw2 + b_pw2


def diff_attention(x, w_qkv, lq1, lk1, lq2, lk2, subln_g, w_o, lambda_init):
    b, s, _ = x.shape
    q, k, v = jnp.split(x @ w_qkv, 3, axis=-1)
    q = q.reshape(b, s, DIFF_HEADS, 2, DIFF_HEAD_DIM)
    k = k.reshape(b, s, DIFF_HEADS, 2, DIFF_HEAD_DIM)
    v = v.reshape(b, s, DIFF_HEADS, 2 * DIFF_HEAD_DIM)
    f32 = jnp.float32
    lam = (jnp.exp(jnp.sum(lq1.astype(f32) * lk1.astype(f32)))
           - jnp.exp(jnp.sum(lq2.astype(f32) * lk2.astype(f32))) + lambda_init)
    nb = s // Q_BLOCK
    q_blocks = q.reshape(b, nb, Q_BLOCK, DIFF_HEADS, 2, DIFF_HEAD_DIM).transpose(1, 0, 2, 3, 4, 5)
    q_pos = jnp.arange(s).reshape(nb, Q_BLOCK)
    k_pos = jnp.arange(s)
    scale = DIFF_HEAD_DIM ** -0.5

    def attend_block(args):
        qb, qp = args
        sc = jnp.einsum("bqhcd,bkhcd->bhcqk", qb, k).astype(f32) * scale
        sc = jnp.where(qp[:, None] >= k_pos[None, :], sc, MASK_VALUE)
        pr = jax.nn.softmax(sc, axis=-1)
        wts = pr[:, :, 0] - lam * pr[:, :, 1]
        return jnp.einsum("bhqk,bkhe->bqhe", wts.astype(v.dtype), v)

    o = lax.map(attend_block, (q_blocks, q_pos))
    o = o.transpose(1, 0, 2, 3, 4).reshape(b, s, DIFF_HEADS, 2 * DIFF_HEAD_DIM)
    o = rms_norm(o, subln_g) * (1.0 - lambda_init)
    return o.reshape(b, s, D_MODEL) @ w_o


def conv_ffn(x, w_in, w_dw, w_out):
    u = causal_dwconv(x @ w_in, w_dw)
    g, up = jnp.split(u, 2, axis=-1)
    return (jax.nn.silu(g) * up) @ w_out


def per_layer_embed(h, p_i, pre_g, w_gate, b_gate, w_proj, post_g):
    gate = jax.nn.sigmoid(rms_norm(h, pre_g) @ w_gate + b_gate)
    return rms_norm(gate * (p_i @ w_proj), post_g)


def setup_inputs(seed: int = 0) -> dict:
    key = jax.random.key(seed)
    ks = jax.random.split(key, 32)
    D, F, dh = D_MODEL, D_FF, DIFF_HEAD_DIM
    na, nb = N_CONV_LAYERS, N_ATTN_LAYERS
    nrm = lambda k, shape, s: jax.random.normal(k, shape, jnp.float32) * s
    gain = lambda k, shape: 1.0 + 0.05 * jax.random.normal(k, shape, jnp.float32)
    return {
        "x": nrm(ks[0], (BATCH, SEQ, D), 1.0),
        "p": nrm(ks[1], (DEPTH, BATCH, SEQ, PLE_DIM), 1.0),
        "pre_mix_g": gain(ks[2], (DEPTH, D)),
        "post_mix_g": gain(ks[3], (DEPTH, D)),
        "conv_w_pw1": nrm(ks[4], (na, D, 2 * D), D ** -0.5),
        "conv_b_pw1": nrm(ks[5], (na, 2 * D), 0.02),
        "conv_w_dw": nrm(ks[6], (na, CONV_K, D), CONV_K ** -0.5),
        "conv_b_dw": nrm(ks[7], (na, D), 0.02),
        "conv_ln_g": gain(ks[8], (na, D)),
        "conv_ln_b": nrm(ks[9], (na, D), 0.02),
        "conv_w_pw2": nrm(ks[10], (na, D, D), D ** -0.5),
        "conv_b_pw2": nrm(ks[11], (na, D), 0.02),
        "attn_w_qkv": nrm(ks[12], (nb, D, 3 * D), D ** -0.5),
        "attn_lambda_q1": nrm(ks[13], (nb, dh), 0.1),
        "attn_lambda_k1": nrm(ks[14], (nb, dh), 0.1),
        "attn_lambda_q2": nrm(ks[15], (nb, dh), 0.1),
        "attn_lambda_k2": nrm(ks[16], (nb, dh), 0.1),
        "attn_subln_g": gain(ks[17], (nb, 2 * dh)),
        "attn_w_o": nrm(ks[18], (nb, D, D), D ** -0.5),
        "pre_ffn_g": gain(ks[19], (DEPTH, D)),
        "post_ffn_g": gain(ks[20], (DEPTH, D)),
        "ffn_w_in": nrm(ks[21], (DEPTH, D, 2 * F), D ** -0.5),
        "ffn_w_dw": nrm(ks[22], (DEPTH, FFN_CONV_K, 2 * F), FFN_CONV_K ** -0.5),
        "ffn_w_out": nrm(ks[23], (DEPTH, F, D), F ** -0.5),
        "ple_pre_g": gain(ks[24], (DEPTH, D)),
        "ple_w_gate": nrm(ks[25], (DEPTH, D, D), D ** -0.5),
        "ple_b_gate": nrm(ks[26], (DEPTH, D), 0.02),
        "ple_w_proj": nrm(ks[27], (DEPTH, PLE_DIM, D), PLE_DIM ** -0.5),
        "ple_post_g": gain(ks[28], (DEPTH, D)),
    }


def reference(x, p, pre_mix_g, post_mix_g, conv_w_pw1, conv_b_pw1, conv_w_dw, conv_b_dw,
              conv_ln_g, conv_ln_b, conv_w_pw2, conv_b_pw2, attn_w_qkv, attn_lambda_q1,
              attn_lambda_k1, attn_lambda_q2, attn_lambda_k2, attn_subln_g, attn_w_o,
              pre_ffn_g, post_ffn_g, ffn_w_in, ffn_w_dw, ffn_w_out, ple_pre_g, ple_w_gate,
              ple_b_gate, ple_w_proj, ple_post_g):
    h = x
    for i in range(DEPTH):
        hn = rms_norm(h, pre_mix_g[i])
        j = i // N_MIXERS
        if i % N_MIXERS == 0:
            y = conformer_conv(hn, conv_w_pw1[j], conv_b_pw1[j], conv_w_dw[j], conv_b_dw[j],
                               conv_ln_g[j], conv_ln_b[j], conv_w_pw2[j], conv_b_pw2[j])
        else:
            y = diff_attention(hn, attn_w_qkv[j], attn_lambda_q1[j], attn_lambda_k1[j],
                               attn_lambda_q2[j], attn_lambda_k2[j], attn_subln_g[j],
                               attn_w_o[j], lambda_init_fn(i))
        h = h + rms_norm(y, post_mix_g[i])
        f = conv_ffn(rms_norm(h, pre_ffn_g[i]), ffn_w_in[i], ffn_w_dw[i], ffn_w_out[i])
        h = h + rms_norm(f, post_ffn_g[i])
        h = h + per_layer_embed(h, p[i], ple_pre_g[i], ple_w_gate[i], ple_b_gate[i],
                                ple_w_proj[i], ple_post_g[i])
    return h
```

```python
import functools
import math

import jax
import jax.numpy as jnp
from jax import lax
from jax.experimental import pallas as pl
from jax.experimental.pallas import tpu as pltpu

F32 = jnp.float32
BF16 = jnp.bfloat16

RMS_EPS = 1e-6
LN_EPS = 1e-5
MASK_VALUE = -1e30
HEAD_DIM = 128
CONV_HALO = 32
FFN_HALO = 16
SUBLANES = 8
LANES = 128
V7X_VMEM_LIMIT_BYTES = 56 * 1024 * 1024


def _params(n_axes, vmem_bytes=V7X_VMEM_LIMIT_BYTES):
    return pltpu.CompilerParams(dimension_semantics=("arbitrary",) * n_axes,
                                vmem_limit_bytes=vmem_bytes)


def _resident(shape):
    return pl.BlockSpec(shape, lambda *_: (0,) * len(shape), pipeline_mode=pl.Buffered(1))


def _rms_rows(x, g):
    ms = jnp.mean(x * x, axis=-1, keepdims=True)
    return x * lax.rsqrt(ms + RMS_EPS) * g


def _sigmoid(x):
    return 1.0 / (1.0 + jnp.exp(-x))


def _dot(a, b):
    return jnp.dot(a, b, preferred_element_type=F32)


def _norm_to_scratch(dst_ref, src_ref, g, rows, dst_off=0, chunk=256):
    chunk = min(chunk, rows)

    def body(c, carry):
        r = pl.multiple_of(c * chunk, chunk)
        x = src_ref[pl.ds(r, chunk), :]
        dst_ref[pl.ds(dst_off + r, chunk), :] = _rms_rows(x, g).astype(dst_ref.dtype)
        return carry

    lax.fori_loop(0, rows // chunk, body, 0)


def _norm_glu_kernel(x_ref, g_ref, wa_ref, wg_ref, ba_ref, bg_ref, o_ref, xn_ref):
    @pl.when(pl.program_id(1) == 0)
    def _():
        _norm_to_scratch(xn_ref, x_ref, g_ref[...], x_ref.shape[0])

    xn = xn_ref[...]
    a = _dot(xn, wa_ref[...]) + ba_ref[...]
    gate = _dot(xn, wg_ref[...]) + bg_ref[...]
    o_ref[...] = (a * _sigmoid(gate)).astype(o_ref.dtype)


def _norm_glu(x, g, w, b, tm, tn):
    s, d = x.shape
    n = w.shape[1] // 2
    nj = n // tn
    return pl.pallas_call(
        _norm_glu_kernel,
        grid=(s // tm, nj),
        in_specs=[
            pl.BlockSpec((tm, d), lambda i, j: (i, 0)),
            pl.BlockSpec((1, d), lambda i, j: (0, 0)),
            pl.BlockSpec((d, tn), lambda i, j: (0, j)),
            pl.BlockSpec((d, tn), lambda i, j: (0, j + nj)),
            pl.BlockSpec((1, tn), lambda i, j: (0, j)),
            pl.BlockSpec((1, tn), lambda i, j: (0, j + nj)),
        ],
        out_specs=pl.BlockSpec((tm, tn), lambda i, j: (i, j)),
        out_shape=jax.ShapeDtypeStruct((s, n), F32),
        scratch_shapes=[pltpu.VMEM((tm, d), BF16)],
        compiler_params=_params(2),
        name="norm_glu",
    )(x, g, w, w, b, b)


def _norm_matmul_kernel(x_ref, g_ref, w_ref, o_ref, xn_ref):
    @pl.when(pl.program_id(1) == 0)
    def _():
        _norm_to_scratch(xn_ref, x_ref, g_ref[...], x_ref.shape[0])

    o_ref[...] = _dot(xn_ref[...], w_ref[...]).astype(o_ref.dtype)


def _norm_matmul(x, g, w, tm, tn, out_dtype):
    s, d = x.shape
    n = w.shape[1]
    return pl.pallas_call(
        _norm_matmul_kernel,
        grid=(s // tm, n // tn),
        in_specs=[
            pl.BlockSpec((tm, d), lambda i, j: (i, 0)),
            pl.BlockSpec((1, d), lambda i, j: (0, 0)),
            pl.BlockSpec((d, tn), lambda i, j: (0, j)),
        ],
        out_specs=pl.BlockSpec((tm, tn), lambda i, j: (i, j)),
        out_shape=jax.ShapeDtypeStruct((s, n), out_dtype),
        scratch_shapes=[pltpu.VMEM((tm, d), BF16)],
        compiler_params=_params(2),
        name="norm_matmul",
    )(x, g, w)


def _dwconv_window(win, w_ref, lane0, rows, k):
    first = CONV_HALO - (k - 1)
    acc = None
    for b in range(SUBLANES):
        offs = [o for o in range(b, CONV_HALO + 1, SUBLANES) if 0 <= o - first < k]
        if not offs:
            continue
        shifted = win if b == 0 else win[b:b + rows + CONV_HALO - SUBLANES]
        for o in offs:
            j = o - first
            term = shifted[o - b:o - b + rows] * w_ref[j:j + 1, pl.ds(lane0, LANES)]
            acc = term if acc is None else acc + term
    return acc


def _conv_mix_kernel(u_ref, halo_ref, h_ref, wdw_ref, bdw_ref, lng_ref, lnb_ref, w2_ref, b2_ref,
                     pg_ref, o_ref, ext_ref, act_ref, *, k, rows):
    tm, d = u_ref.shape
    ext_ref[0:CONV_HALO, :] = jnp.where(pl.program_id(0) > 0, halo_ref[...], 0.0)
    ext_ref[CONV_HALO:, :] = u_ref[...]

    n_lane = d // LANES

    def conv_body(c, carry):
        r0 = pl.multiple_of((c // n_lane) * rows, rows)
        lane0 = pl.multiple_of((c % n_lane) * LANES, LANES)
        win = ext_ref[pl.ds(r0, rows + CONV_HALO), pl.ds(lane0, LANES)]
        y = _dwconv_window(win, wdw_ref, lane0, rows, k) + bdw_ref[:, pl.ds(lane0, LANES)]
        ext_ref[pl.ds(r0, rows), pl.ds(lane0, LANES)] = y
        return carry

    lax.fori_loop(0, (tm // rows) * n_lane, conv_body, 0)

    chunk = min(256, tm)

    def ln_body(c, carry):
        r = pl.multiple_of(c * chunk, chunk)
        x = ext_ref[pl.ds(r, chunk), :]
        mu = jnp.mean(x, axis=-1, keepdims=True)
        xc = x - mu
        y = xc * lax.rsqrt(jnp.mean(xc * xc, axis=-1, keepdims=True) + LN_EPS)
        y = y * lng_ref[...] + lnb_ref[...]
        act_ref[pl.ds(r, chunk), :] = (y * _sigmoid(y)).astype(act_ref.dtype)
        return carry

    lax.fori_loop(0, tm // chunk, ln_body, 0)

    y = _dot(act_ref[...], w2_ref[...]) + b2_ref[...]
    o_ref[...] = h_ref[...] + _rms_rows(y, pg_ref[...])


def _conv_mix(u, h, w_dw, b_dw, ln_g, ln_b, w2, b2, post_g, tm):
    s, d = u.shape
    k = w_dw.shape[0]
    halo_blocks = tm // CONV_HALO
    row = lambda i: (i, 0)
    return pl.pallas_call(
        functools.partial(_conv_mix_kernel, k=k, rows=64),
        grid=(s // tm,),
        in_specs=[
            pl.BlockSpec((tm, d), row),
            pl.BlockSpec((CONV_HALO, d), lambda i: (jnp.maximum(i * halo_blocks - 1, 0), 0)),
            pl.BlockSpec((tm, d), row),
            _resident((k, d)),
            _resident((1, d)),
            _resident((1, d)),
            _resident((1, d)),
            _resident((d, d)),
            _resident((1, d)),
            _resident((1, d)),
        ],
        out_specs=pl.BlockSpec((tm, d), row),
        out_shape=jax.ShapeDtypeStruct((s, d), F32),
        scratch_shapes=[pltpu.VMEM((tm + CONV_HALO, d), F32), pltpu.VMEM((tm, d), BF16)],
        compiler_params=_params(1),
        name="conv_mix",
    )(u, u, h, w_dw, b_dw, ln_g, ln_b, w2, b2, post_g)


def _causal3(u, w, tm):
    return (u[FFN_HALO:] * w[2:3]
            + u[FFN_HALO - 1:FFN_HALO - 1 + tm] * w[1:2]
            + u[FFN_HALO - 2:FFN_HALO - 2 + tm] * w[0:1])


def _ffn_kernel(h_ref, halo_ref, g_ref, wg_ref, wu_ref, dwg_ref, dwu_ref, wo_ref, pg_ref,
                o_ref, xn_ref):
    i = pl.program_id(0)
    j = pl.program_id(1)
    tm = h_ref.shape[0]

    @pl.when(j == 0)
    def _():
        hist = jnp.where(i > 0, _rms_rows(halo_ref[...], g_ref[...]), 0.0)
        xn_ref[0:FFN_HALO, :] = hist.astype(xn_ref.dtype)
        _norm_to_scratch(xn_ref, h_ref, g_ref[...], tm, dst_off=FFN_HALO)

    xn = xn_ref[...]
    gate = _causal3(_dot(xn, wg_ref[...]), dwg_ref[...], tm)
    up = _causal3(_dot(xn, wu_ref[...]), dwu_ref[...], tm)
    act = (gate * _sigmoid(gate) * up).astype(BF16)
    part = _dot(act, wo_ref[...])

    @pl.when(j == 0)
    def _():
        o_ref[...] = part

    @pl.when(j > 0)
    def _():
        o_ref[...] += part

    @pl.when(j == pl.num_programs(1) - 1)
    def _():
        chunk = min(256, tm)

        def body(c, carry):
            r = pl.multiple_of(c * chunk, chunk)
            rows = pl.ds(r, chunk)
            o_ref[rows, :] = h_ref[rows, :] + _rms_rows(o_ref[rows, :], pg_ref[...])
            return carry

        lax.fori_loop(0, tm // chunk, body, 0)


def _conv_ffn(h, pre_g, w_in, w_dw, w_out, post_g, tm, tf):
    s, d = h.shape
    f = w_out.shape[0]
    nj = f // tf
    halo_blocks = tm // FFN_HALO
    return pl.pallas_call(
        _ffn_kernel,
        grid=(s // tm, nj),
        in_specs=[
            pl.BlockSpec((tm, d), lambda i, j: (i, 0), pipeline_mode=pl.Buffered(1)),
            pl.BlockSpec((FFN_HALO, d), lambda i, j: (jnp.maximum(i * halo_blocks - 1, 0), 0)),
            pl.BlockSpec((1, d), lambda i, j: (0, 0)),
            pl.BlockSpec((d, tf), lambda i, j: (0, j)),
            pl.BlockSpec((d, tf), lambda i, j: (0, j + nj)),
            pl.BlockSpec((3, tf), lambda i, j: (0, j)),
            pl.BlockSpec((3, tf), lambda i, j: (0, j + nj)),
            pl.BlockSpec((tf, d), lambda i, j: (j, 0)),
            pl.BlockSpec((1, d), lambda i, j: (0, 0)),
        ],
        out_specs=pl.BlockSpec((tm, d), lambda i, j: (i, 0)),
        out_shape=jax.ShapeDtypeStruct((s, d), F32),
        scratch_shapes=[pltpu.VMEM((tm + FFN_HALO, d), BF16)],
        compiler_params=_params(2),
        name="conv_ffn",
    )(h, h, pre_g, w_in, w_in, w_dw, w_dw, w_out, post_g)


def _ple_kernel(h_ref, p_ref, g_ref, wg_ref, bg_ref, wp_ref, pg_ref, o_ref):
    h = h_ref[...]
    xn = _rms_rows(h, g_ref[...]).astype(BF16)
    gate = _sigmoid(_dot(xn, wg_ref[...]) + bg_ref[...])
    emb = _dot(p_ref[...].astype(BF16), wp_ref[...])
    o_ref[...] = h + _rms_rows(gate * emb, pg_ref[...])


def _per_layer_embed(h, p, pre_g, w_gate, b_gate, w_proj, post_g, tm):
    s, d = h.shape
    e = p.shape[1]
    row = lambda i: (i, 0)
    return pl.pallas_call(
        _ple_kernel,
        grid=(s // tm,),
        in_specs=[
            pl.BlockSpec((tm, d), row),
            pl.BlockSpec((tm, e), row),
            _resident((1, d)),
            _resident((d, d)),
            _resident((1, d)),
            _resident((e, d)),
            _resident((1, d)),
        ],
        out_specs=pl.BlockSpec((tm, d), row),
        out_shape=jax.ShapeDtypeStruct((s, d), F32),
        compiler_params=_params(1),
        name="per_layer_embed",
    )(h, p, pre_g, w_gate, b_gate, w_proj, post_g)


def _diff_attn_kernel(q_ref, k_ref, v_ref, lq1_ref, lk1_ref, lq2_ref, lk2_ref, sg_ref, o_ref,
                      m_ref, l_ref, acc_ref, *, scale, lambda_init):
    qi = pl.program_id(1)
    tq = q_ref.shape[0]
    m_ref[...] = jnp.full(m_ref.shape, MASK_VALUE, F32)
    l_ref[...] = jnp.zeros(l_ref.shape, F32)
    acc_ref[...] = jnp.zeros(acc_ref.shape, F32)

    def step(k0, diagonal):
        v = v_ref[pl.ds(k0, tq), :]
        for c in range(2):
            lanes = slice(c * HEAD_DIM, (c + 1) * HEAD_DIM)
            s = lax.dot_general(q_ref[:, lanes], k_ref[pl.ds(k0, tq), lanes],
                                (((1,), (1,)), ((), ())), preferred_element_type=F32)
            if diagonal:
                rows = lax.broadcasted_iota(jnp.int32, s.shape, 0)
                cols = lax.broadcasted_iota(jnp.int32, s.shape, 1)
                s = jnp.where(rows >= cols, s, MASK_VALUE)
            m_old = m_ref[c]
            m_new = jnp.maximum(m_old, jnp.max(s, axis=1, keepdims=True))
            p = jnp.exp((s - m_new) * scale)
            alpha = jnp.exp((m_old - m_new) * scale)
            l_ref[c] = alpha * l_ref[c] + jnp.sum(p, axis=1, keepdims=True)
            acc_ref[c] = alpha * acc_ref[c] + _dot(p.astype(BF16), v)
            m_ref[c] = m_new

    def body(kb, carry):
        step(pl.multiple_of(kb * tq, tq), False)
        return carry

    lax.fori_loop(0, qi, body, 0)
    step(pl.multiple_of(qi * tq, tq), True)

    lam = (jnp.exp(jnp.sum(lq1_ref[...] * lk1_ref[...], keepdims=True))
           - jnp.exp(jnp.sum(lq2_ref[...] * lk2_ref[...], keepdims=True)) + lambda_init)
    o = acc_ref[0] / l_ref[0] - lam * (acc_ref[1] / l_ref[1])
    o_ref[...] = (_rms_rows(o, sg_ref[...]) * (1.0 - lambda_init)).astype(o_ref.dtype)


def _diff_attention(qkv, lq1, lk1, lq2, lk2, subln_g, lambda_init, tq):
    s = qkv.shape[0]
    vd = 2 * HEAD_DIM
    heads = qkv.shape[1] // (3 * vd)
    vec = pl.BlockSpec((1, HEAD_DIM), lambda h, i: (0, 0))
    return pl.pallas_call(
        functools.partial(_diff_attn_kernel, scale=HEAD_DIM ** -0.5, lambda_init=lambda_init),
        grid=(heads, s // tq),
        in_specs=[
            pl.BlockSpec((tq, vd), lambda h, i: (i, h)),
            pl.BlockSpec((s, vd), lambda h, i: (0, heads + h)),
            pl.BlockSpec((s, vd), lambda h, i: (0, 2 * heads + h)),
            vec, vec, vec, vec,
            pl.BlockSpec((1, vd), lambda h, i: (0, 0)),
        ],
        out_specs=pl.BlockSpec((tq, vd), lambda h, i: (i, h)),
        out_shape=jax.ShapeDtypeStruct((s, heads * vd), BF16),
        scratch_shapes=[pltpu.VMEM((2, tq, 1), F32), pltpu.VMEM((2, tq, 1), F32),
                        pltpu.VMEM((2, tq, vd), F32)],
        compiler_params=_params(2),
        name="diff_attention",
    )(qkv, qkv, qkv, lq1, lk1, lq2, lk2, subln_g)


def _proj_residual_kernel(a_ref, h_ref, w_ref, pg_ref, o_ref):
    o_ref[...] = h_ref[...] + _rms_rows(_dot(a_ref[...], w_ref[...]), pg_ref[...])


def _proj_residual(a, h, w, post_g, tm):
    s, d = h.shape
    row = lambda i: (i, 0)
    return pl.pallas_call(
        _proj_residual_kernel,
        grid=(s // tm,),
        in_specs=[pl.BlockSpec((tm, a.shape[1]), row), pl.BlockSpec((tm, d), row),
                  _resident(w.shape), _resident((1, d))],
        out_specs=pl.BlockSpec((tm, d), row),
        out_shape=jax.ShapeDtypeStruct((s, d), F32),
        compiler_params=_params(1),
        name="proj_residual",
    )(a, h, w, post_g)


def _tiles(s, d, f):
    ffn_cols = 512 if f % 512 == 0 else 256
    return dict(
        proj_rows=min(1024, s), proj_cols=min(512, d),
        conv_rows=min(256, s),
        ffn_rows=min(1024, s), ffn_cols=min(ffn_cols, f),
        ple_rows=min(256, s),
        attn_rows=min(512, s),
        out_rows=min(512, s),
    )


def kernel(x, p, pre_mix_g, post_mix_g, conv_w_pw1, conv_b_pw1, conv_w_dw, conv_b_dw, conv_ln_g, conv_ln_b, conv_w_pw2, conv_b_pw2, attn_w_qkv, attn_lambda_q1, attn_lambda_k1, attn_lambda_q2, attn_lambda_k2, attn_subln_g, attn_w_o, pre_ffn_g, post_ffn_g, ffn_w_in, ffn_w_dw, ffn_w_out, ple_pre_g, ple_w_gate, ple_b_gate, ple_w_proj, ple_post_g):
    batch, s, d = x.shape
    depth = p.shape[0]
    f = ffn_w_out.shape[1]
    t = _tiles(s, d, f)
    vec = lambda a: a.reshape(1, -1)
    bf = lambda a: a.astype(BF16)

    outs = []
    for b in range(batch):
        h = x[b]
        for i in range(depth):
            j = i // 2
            if i % 2 == 0:
                u = _norm_glu(h, vec(pre_mix_g[i]), bf(conv_w_pw1[j]), vec(conv_b_pw1[j]),
                              t["proj_rows"], t["proj_cols"])
                h = _conv_mix(u, h, conv_w_dw[j], vec(conv_b_dw[j]), vec(conv_ln_g[j]),
                              vec(conv_ln_b[j]), bf(conv_w_pw2[j]), vec(conv_b_pw2[j]),
                              vec(post_mix_g[i]), t["conv_rows"])
            else:
                qkv = _norm_matmul(h, vec(pre_mix_g[i]), bf(attn_w_qkv[j]),
                                   t["proj_rows"], t["proj_cols"], BF16)
                lambda_init = 0.8 - 0.6 * math.exp(-0.3 * i)
                o = _diff_attention(qkv, vec(attn_lambda_q1[j]), vec(attn_lambda_k1[j]),
                                    vec(attn_lambda_q2[j]), vec(attn_lambda_k2[j]),
                                    vec(attn_subln_g[j]), lambda_init, t["attn_rows"])
                h = _proj_residual(o, h, bf(attn_w_o[j]), vec(post_mix_g[i]), t["out_rows"])
            h = _conv_ffn(h, vec(pre_ffn_g[i]), bf(ffn_w_in[i]), ffn_w_dw[i], bf(ffn_w_out[i]),
                          vec(post_ffn_g[i]), t["ffn_rows"], t["ffn_cols"])
            h = _per_layer_embed(h, p[i, b], vec(ple_pre_g[i]), bf(ple_w_gate[i]),
                                 vec(ple_b_gate[i]), bf(ple_w_proj[i]), vec(ple_post_g[i]),
                                 t["ple_rows"])
        outs.append(h)
    return jnp.stack(outs)
```

```python
import functools
import math

import jax
import jax.numpy as jnp
from jax import lax
from jax.experimental import pallas as pl
from jax.experimental.pallas import tpu as pltpu

F32 = jnp.float32
BF16 = jnp.bfloat16

RMS_EPS = 1e-6
LN_EPS = 1e-5
MASK_VALUE = -1e30
HEAD_DIM = 128
CONV_HALO = 32
FFN_HALO = 16
SUBLANES = 8
LANES = 128
V7X_VMEM_LIMIT_BYTES = 56 * 1024 * 1024


def _params(n_axes, vmem_bytes=V7X_VMEM_LIMIT_BYTES):
    return pltpu.CompilerParams(dimension_semantics=("arbitrary",) * n_axes,
                                vmem_limit_bytes=vmem_bytes)


def _resident(shape):
    return pl.BlockSpec(shape, lambda *_: (0,) * len(shape), pipeline_mode=pl.Buffered(1))


def _rms_rows(x, g):
    ms = jnp.mean(x * x, axis=-1, keepdims=True)
    return x * lax.rsqrt(ms + RMS_EPS) * g


def _sigmoid(x):
    return 1.0 / (1.0 + jnp.exp(-x))


def _dot(a, b):
    return jnp.dot(a, b, preferred_element_type=F32)


def _norm_to_scratch(dst_ref, src_ref, g, rows, dst_off=0, chunk=256):
    chunk = min(chunk, rows)

    def body(c, carry):
        r = pl.multiple_of(c * chunk, chunk)
        x = src_ref[pl.ds(r, chunk), :]
        dst_ref[pl.ds(dst_off + r, chunk), :] = _rms_rows(x, g).astype(dst_ref.dtype)
        return carry

    lax.fori_loop(0, rows // chunk, body, 0)


def _norm_glu_kernel(x_ref, g_ref, wa_ref, wg_ref, ba_ref, bg_ref, o_ref, xn_ref):
    @pl.when(pl.program_id(1) == 0)
    def _():
        _norm_to_scratch(xn_ref, x_ref, g_ref[...], x_ref.shape[0])

    xn = xn_ref[...]
    a = _dot(xn, wa_ref[...]) + ba_ref[...]
    gate = _dot(xn, wg_ref[...]) + bg_ref[...]
    o_ref[...] = (a * _sigmoid(gate)).astype(o_ref.dtype)


def _norm_glu(x, g, w, b, tm, tn):
    s, d = x.shape
    n = w.shape[1] // 2
    nj = n // tn
    return pl.pallas_call(
        _norm_glu_kernel,
        grid=(s // tm, nj),
        in_specs=[
            pl.BlockSpec((tm, d), lambda i, j: (i, 0)),
            pl.BlockSpec((1, d), lambda i, j: (0, 0)),
            pl.BlockSpec((d, tn), lambda i, j: (0, j)),
            pl.BlockSpec((d, tn), lambda i, j: (0, j + nj)),
            pl.BlockSpec((1, tn), lambda i, j: (0, j)),
            pl.BlockSpec((1, tn), lambda i, j: (0, j + nj)),
        ],
        out_specs=pl.BlockSpec((tm, tn), lambda i, j: (i, j)),
        out_shape=jax.ShapeDtypeStruct((s, n), F32),
        scratch_shapes=[pltpu.VMEM((tm, d), BF16)],
        compiler_params=_params(2),
        name="norm_glu",
    )(x, g, w, w, b, b)


def _norm_matmul_kernel(x_ref, g_ref, w_ref, cs_ref, o_ref, xn_ref):
    @pl.when(pl.program_id(1) == 0)
    def _():
        _norm_to_scratch(xn_ref, x_ref, g_ref[...], x_ref.shape[0])

    o_ref[...] = (_dot(xn_ref[...], w_ref[...]) * cs_ref[...]).astype(o_ref.dtype)


def _norm_matmul(x, g, w, col_scale, tm, tn, out_dtype):
    s, d = x.shape
    n = w.shape[1]
    return pl.pallas_call(
        _norm_matmul_kernel,
        grid=(s // tm, n // tn),
        in_specs=[
            pl.BlockSpec((tm, d), lambda i, j: (i, 0)),
            pl.BlockSpec((1, d), lambda i, j: (0, 0)),
            pl.BlockSpec((d, tn), lambda i, j: (0, j)),
            pl.BlockSpec((1, tn), lambda i, j: (0, j)),
        ],
        out_specs=pl.BlockSpec((tm, tn), lambda i, j: (i, j)),
        out_shape=jax.ShapeDtypeStruct((s, n), out_dtype),
        scratch_shapes=[pltpu.VMEM((tm, d), BF16)],
        compiler_params=_params(2),
        name="norm_matmul",
    )(x, g, w, col_scale)


def _dwconv_window(win, w8_ref, lanes, rows, k):
    first = CONV_HALO - (k - 1)
    n_win = rows + CONV_HALO
    acc = None
    for b in range(SUBLANES):
        offs = [o for o in range(b, CONV_HALO + 1, SUBLANES) if 0 <= o - first < k]
        if not offs:
            continue
        shifted = win if b == 0 else pltpu.roll(win, n_win - b, axis=0)
        for o in offs:
            tap = pltpu.repeat(w8_ref[o - first, :, lanes], rows // SUBLANES, axis=0)
            term = shifted[o - b:o - b + rows] * tap
            acc = term if acc is None else acc + term
    return acc


def _conv_mix_kernel(u_ref, halo_ref, h_ref, wdw_ref, bdw_ref, lng_ref, lnb_ref, w2_ref, b2_ref,
                     pg_ref, o_ref, ext_ref, act_ref, w8_ref, *, k, rows):
    tm, d = u_ref.shape
    ext_ref[0:CONV_HALO, :] = jnp.where(pl.program_id(0) > 0, halo_ref[...], 0.0)
    ext_ref[CONV_HALO:, :] = u_ref[...]
    for j in range(k):
        w8_ref[j] = jnp.broadcast_to(wdw_ref[j:j + 1, :], (SUBLANES, d))

    n_lane = d // LANES

    def conv_body(c, carry):
        r0 = pl.multiple_of((c // n_lane) * rows, rows)
        lanes = pl.ds(pl.multiple_of((c % n_lane) * LANES, LANES), LANES)
        win = ext_ref[pl.ds(r0, rows + CONV_HALO), lanes]
        y = _dwconv_window(win, w8_ref, lanes, rows, k) + bdw_ref[:, lanes]
        ext_ref[pl.ds(r0, rows), lanes] = y
        return carry

    lax.fori_loop(0, (tm // rows) * n_lane, conv_body, 0)

    chunk = min(256, tm)

    def ln_body(c, carry):
        r = pl.multiple_of(c * chunk, chunk)
        x = ext_ref[pl.ds(r, chunk), :]
        mu = jnp.mean(x, axis=-1, keepdims=True)
        xc = x - mu
        y = xc * lax.rsqrt(jnp.mean(xc * xc, axis=-1, keepdims=True) + LN_EPS)
        y = y * lng_ref[...] + lnb_ref[...]
        act_ref[pl.ds(r, chunk), :] = (y * _sigmoid(y)).astype(act_ref.dtype)
        return carry

    lax.fori_loop(0, tm // chunk, ln_body, 0)

    y = _dot(act_ref[...], w2_ref[...]) + b2_ref[...]
    o_ref[...] = h_ref[...] + _rms_rows(y, pg_ref[...])


def _conv_mix(u, h, w_dw, b_dw, ln_g, ln_b, w2, b2, post_g, tm):
    s, d = u.shape
    k = w_dw.shape[0]
    halo_blocks = tm // CONV_HALO
    row = lambda i: (i, 0)
    return pl.pallas_call(
        functools.partial(_conv_mix_kernel, k=k, rows=min(128, tm)),
        grid=(s // tm,),
        in_specs=[
            pl.BlockSpec((tm, d), row),
            pl.BlockSpec((CONV_HALO, d), lambda i: (jnp.maximum(i * halo_blocks - 1, 0), 0)),
            pl.BlockSpec((tm, d), row),
            _resident((k, d)),
            _resident((1, d)),
            _resident((1, d)),
            _resident((1, d)),
            _resident((d, d)),
            _resident((1, d)),
            _resident((1, d)),
        ],
        out_specs=pl.BlockSpec((tm, d), row),
        out_shape=jax.ShapeDtypeStruct((s, d), F32),
        scratch_shapes=[pltpu.VMEM((tm + CONV_HALO, d), F32), pltpu.VMEM((tm, d), BF16),
                        pltpu.VMEM((k, SUBLANES, d), F32)],
        compiler_params=_params(1),
        name="conv_mix",
    )(u, u, h, w_dw, b_dw, ln_g, ln_b, w2, b2, post_g)


def _causal3(u, w, tm):
    return (u[FFN_HALO:] * w[2:3]
            + u[FFN_HALO - 1:FFN_HALO - 1 + tm] * w[1:2]
            + u[FFN_HALO - 2:FFN_HALO - 2 + tm] * w[0:1])


def _ffn_kernel(h_ref, halo_ref, g_ref, wg_ref, wu_ref, dwg_ref, dwu_ref, wo_ref, pg_ref,
                o_ref, xn_ref):
    i = pl.program_id(0)
    j = pl.program_id(1)
    tm = h_ref.shape[0]

    @pl.when(j == 0)
    def _():
        hist = jnp.where(i > 0, _rms_rows(halo_ref[...], g_ref[...]), 0.0)
        xn_ref[0:FFN_HALO, :] = hist.astype(xn_ref.dtype)
        _norm_to_scratch(xn_ref, h_ref, g_ref[...], tm, dst_off=FFN_HALO)
        o_ref[...] = jnp.zeros(o_ref.shape, o_ref.dtype)

    xn = xn_ref[...]
    gate = _causal3(_dot(xn, wg_ref[...]), dwg_ref[...], tm)
    up = _causal3(_dot(xn, wu_ref[...]), dwu_ref[...], tm)
    act = (gate * _sigmoid(gate) * up).astype(BF16)
    o_ref[...] += _dot(act, wo_ref[...])

    @pl.when(j == pl.num_programs(1) - 1)
    def _():
        chunk = min(256, tm)

        def body(c, carry):
            r = pl.multiple_of(c * chunk, chunk)
            rows = pl.ds(r, chunk)
            o_ref[rows, :] = h_ref[rows, :] + _rms_rows(o_ref[rows, :], pg_ref[...])
            return carry

        lax.fori_loop(0, tm // chunk, body, 0)


def _conv_ffn(h, pre_g, w_in, w_dw, w_out, post_g, tm, tf):
    s, d = h.shape
    f = w_out.shape[0]
    nj = f // tf
    halo_blocks = tm // FFN_HALO
    return pl.pallas_call(
        _ffn_kernel,
        grid=(s // tm, nj),
        in_specs=[
            pl.BlockSpec((tm, d), lambda i, j: (i, 0), pipeline_mode=pl.Buffered(1)),
            pl.BlockSpec((FFN_HALO, d), lambda i, j: (jnp.maximum(i * halo_blocks - 1, 0), 0)),
            pl.BlockSpec((1, d), lambda i, j: (0, 0)),
            pl.BlockSpec((d, tf), lambda i, j: (0, j)),
            pl.BlockSpec((d, tf), lambda i, j: (0, j + nj)),
            pl.BlockSpec((3, tf), lambda i, j: (0, j)),
            pl.BlockSpec((3, tf), lambda i, j: (0, j + nj)),
            pl.BlockSpec((tf, d), lambda i, j: (j, 0)),
            pl.BlockSpec((1, d), lambda i, j: (0, 0)),
        ],
        out_specs=pl.BlockSpec((tm, d), lambda i, j: (i, 0)),
        out_shape=jax.ShapeDtypeStruct((s, d), F32),
        scratch_shapes=[pltpu.VMEM((tm + FFN_HALO, d), BF16)],
        compiler_params=_params(2),
        name="conv_ffn",
    )(h, h, pre_g, w_in, w_in, w_dw, w_dw, w_out, post_g)


def _ple_kernel(h_ref, p_ref, g_ref, wg_ref, bg_ref, wp_ref, pg_ref, o_ref):
    h = h_ref[...]
    xn = _rms_rows(h, g_ref[...]).astype(BF16)
    gate = _sigmoid(_dot(xn, wg_ref[...]) + bg_ref[...])
    emb = _dot(p_ref[...].astype(BF16), wp_ref[...])
    o_ref[...] = h + _rms_rows(gate * emb, pg_ref[...])


def _per_layer_embed(h, p, pre_g, w_gate, b_gate, w_proj, post_g, tm):
    s, d = h.shape
    e = p.shape[1]
    row = lambda i: (i, 0)
    return pl.pallas_call(
        _ple_kernel,
        grid=(s // tm,),
        in_specs=[
            pl.BlockSpec((tm, d), row),
            pl.BlockSpec((tm, e), row),
            _resident((1, d)),
            _resident((d, d)),
            _resident((1, d)),
            _resident((e, d)),
            _resident((1, d)),
        ],
        out_specs=pl.BlockSpec((tm, d), row),
        out_shape=jax.ShapeDtypeStruct((s, d), F32),
        compiler_params=_params(1),
        name="per_layer_embed",
    )(h, p, pre_g, w_gate, b_gate, w_proj, post_g)


def _diff_attn_kernel(q_ref, k_ref, v_ref, lq1_ref, lk1_ref, lq2_ref, lk2_ref, sg_ref, o_ref,
                      m_ref, l_ref, acc_ref, s_ref, mc_ref, *, lambda_init):
    qi = pl.program_id(1)
    tq = q_ref.shape[0]
    tk = tq
    n_lane_tiles = tk // LANES
    m_ref[...] = jnp.full(m_ref.shape, MASK_VALUE, F32)
    l_ref[...] = jnp.zeros(l_ref.shape, F32)
    acc_ref[...] = jnp.zeros(acc_ref.shape, F32)

    def scores(k0, slot, diagonal):
        for c in range(2):
            lanes = slice(c * HEAD_DIM, (c + 1) * HEAD_DIM)
            s = lax.dot_general(q_ref[:, lanes], k_ref[pl.ds(k0, tk), lanes],
                                (((1,), (1,)), ((), ())), preferred_element_type=F32)
            if diagonal:
                rows = lax.broadcasted_iota(jnp.int32, s.shape, 0)
                cols = lax.broadcasted_iota(jnp.int32, s.shape, 1)
                s = jnp.where(rows >= cols, s, MASK_VALUE)
            s_ref[slot, c] = s
            mc_ref[slot, c] = jnp.broadcast_to(jnp.max(s, axis=1, keepdims=True), (tq, LANES))

    def softmax_pv(k0, slot):
        v = v_ref[pl.ds(k0, tk), :]
        for c in range(2):
            m_old = m_ref[c]
            m_new = jnp.maximum(m_old, mc_ref[slot, c])
            p = jnp.exp2(s_ref[slot, c] - pltpu.repeat(m_new, n_lane_tiles, axis=1))
            alpha = jnp.exp2(m_old - m_new)
            psum = p[:, 0:LANES]
            for t in range(1, n_lane_tiles):
                psum = psum + p[:, t * LANES:(t + 1) * LANES]
            l_ref[c] = alpha * l_ref[c] + psum
            acc_ref[c] = (pltpu.repeat(alpha, acc_ref.shape[2] // LANES, axis=1) * acc_ref[c]
                          + _dot(p.astype(BF16), v))
            m_ref[c] = m_new

    def key_start(item):
        return pl.multiple_of((item - 1) * tk, tk)

    diag_start = pl.multiple_of(qi * tk, tk)

    def even_item_start(item):
        return pl.multiple_of(jnp.where(item == 0, diag_start, key_start(item)), tk)

    scores(diag_start, 0, True)

    def pair(j, carry):
        scores(key_start(2 * j + 1), 1, False)
        softmax_pv(even_item_start(2 * j), 0)
        scores(key_start(2 * j + 2), 0, False)
        softmax_pv(key_start(2 * j + 1), 1)
        return carry

    n_pairs = qi // 2
    lax.fori_loop(0, n_pairs, pair, 0)

    @pl.when(qi % 2 == 1)
    def _():
        scores(key_start(qi), 1, False)
        softmax_pv(even_item_start(2 * n_pairs), 0)
        softmax_pv(key_start(qi), 1)

    @pl.when(qi % 2 == 0)
    def _():
        softmax_pv(even_item_start(2 * n_pairs), 0)

    lam = (jnp.exp(jnp.sum(lq1_ref[...] * lk1_ref[...], keepdims=True))
           - jnp.exp(jnp.sum(lq2_ref[...] * lk2_ref[...], keepdims=True)) + lambda_init)
    l0 = jnp.sum(l_ref[0], axis=1, keepdims=True)
    l1 = jnp.sum(l_ref[1], axis=1, keepdims=True)
    o = acc_ref[0] / l0 - lam * (acc_ref[1] / l1)
    o_ref[...] = (_rms_rows(o, sg_ref[...]) * (1.0 - lambda_init)).astype(o_ref.dtype)


def _diff_attention(qkv, lq1, lk1, lq2, lk2, subln_g, lambda_init, tq):
    s = qkv.shape[0]
    vd = 2 * HEAD_DIM
    heads = qkv.shape[1] // (3 * vd)
    vec = pl.BlockSpec((1, HEAD_DIM), lambda h, i: (0, 0))
    return pl.pallas_call(
        functools.partial(_diff_attn_kernel, lambda_init=lambda_init),
        grid=(heads, s // tq),
        in_specs=[
            pl.BlockSpec((tq, vd), lambda h, i: (i, h)),
            pl.BlockSpec((s, vd), lambda h, i: (0, heads + h)),
            pl.BlockSpec((s, vd), lambda h, i: (0, 2 * heads + h)),
            vec, vec, vec, vec,
            pl.BlockSpec((1, vd), lambda h, i: (0, 0)),
        ],
        out_specs=pl.BlockSpec((tq, vd), lambda h, i: (i, h)),
        out_shape=jax.ShapeDtypeStruct((s, heads * vd), BF16),
        scratch_shapes=[pltpu.VMEM((2, tq, LANES), F32), pltpu.VMEM((2, tq, LANES), F32),
                        pltpu.VMEM((2, tq, vd), F32), pltpu.VMEM((2, 2, tq, tq), F32),
                        pltpu.VMEM((2, 2, tq, LANES), F32)],
        compiler_params=_params(2),
        name="diff_attention",
    )(qkv, qkv, qkv, lq1, lk1, lq2, lk2, subln_g)


def _proj_residual_kernel(a_ref, h_ref, w_ref, pg_ref, o_ref):
    o_ref[...] = h_ref[...] + _rms_rows(_dot(a_ref[...], w_ref[...]), pg_ref[...])


def _proj_residual(a, h, w, post_g, tm):
    s, d = h.shape
    row = lambda i: (i, 0)
    return pl.pallas_call(
        _proj_residual_kernel,
        grid=(s // tm,),
        in_specs=[pl.BlockSpec((tm, a.shape[1]), row), pl.BlockSpec((tm, d), row),
                  _resident(w.shape), _resident((1, d))],
        out_specs=pl.BlockSpec((tm, d), row),
        out_shape=jax.ShapeDtypeStruct((s, d), F32),
        compiler_params=_params(1),
        name="proj_residual",
    )(a, h, w, post_g)


def _tiles(s, d, f):
    ffn_cols = 512 if f % 512 == 0 else 256
    return dict(
        proj_rows=min(1024, s), proj_cols=min(512, d),
        conv_rows=min(256, s),
        ffn_rows=min(1024, s), ffn_cols=min(ffn_cols, f),
        ple_rows=min(256, s),
        attn_rows=min(512, s),
        out_rows=min(512, s),
    )


def kernel(x, p, pre_mix_g, post_mix_g, conv_w_pw1, conv_b_pw1, conv_w_dw, conv_b_dw, conv_ln_g, conv_ln_b, conv_w_pw2, conv_b_pw2, attn_w_qkv, attn_lambda_q1, attn_lambda_k1, attn_lambda_q2, attn_lambda_k2, attn_subln_g, attn_w_o, pre_ffn_g, post_ffn_g, ffn_w_in, ffn_w_dw, ffn_w_out, ple_pre_g, ple_w_gate, ple_b_gate, ple_w_proj, ple_post_g):
    batch, s, d = x.shape
    depth = p.shape[0]
    f = ffn_w_out.shape[1]
    t = _tiles(s, d, f)
    vec = lambda a: a.reshape(1, -1)
    bf = lambda a: a.astype(BF16)

    outs = []
    for b in range(batch):
        h = x[b]
        for i in range(depth):
            j = i // 2
            if i % 2 == 0:
                u = _norm_glu(h, vec(pre_mix_g[i]), bf(conv_w_pw1[j]), vec(conv_b_pw1[j]),
                              t["proj_rows"], t["proj_cols"])
                h = _conv_mix(u, h, conv_w_dw[j], vec(conv_b_dw[j]), vec(conv_ln_g[j]),
                              vec(conv_ln_b[j]), bf(conv_w_pw2[j]), vec(conv_b_pw2[j]),
                              vec(post_mix_g[i]), t["conv_rows"])
            else:
                q_scale = HEAD_DIM ** -0.5 * math.log2(math.e)
                col_scale = jnp.concatenate([jnp.full((1, d), q_scale, F32),
                                             jnp.ones((1, 2 * d), F32)], axis=1)
                qkv = _norm_matmul(h, vec(pre_mix_g[i]), bf(attn_w_qkv[j]), col_scale,
                                   t["proj_rows"], t["proj_cols"], BF16)
                lambda_init = 0.8 - 0.6 * math.exp(-0.3 * i)
                o = _diff_attention(qkv, vec(attn_lambda_q1[j]), vec(attn_lambda_k1[j]),
                                    vec(attn_lambda_q2[j]), vec(attn_lambda_k2[j]),
                                    vec(attn_subln_g[j]), lambda_init, t["attn_rows"])
                h = _proj_residual(o, h, bf(attn_w_o[j]), vec(post_mix_g[i]), t["out_rows"])
            h = _conv_ffn(h, vec(pre_ffn_g[i]), bf(ffn_w_in[i]), ffn_w_dw[i], bf(ffn_w_out[i]),
                          vec(post_ffn_g[i]), t["ffn_rows"], t["ffn_cols"])
            h = _per_layer_embed(h, p[i, b], vec(ple_pre_g[i]), bf(ple_w_gate[i]),
                                 vec(ple_b_gate[i]), bf(ple_w_proj[i]), vec(ple_post_g[i]),
                                 t["ple_rows"])
        outs.append(h)
    return jnp.stack(outs)
```

```python
import functools
import math

import jax
import jax.numpy as jnp
from jax import lax
from jax.experimental import pallas as pl
from jax.experimental.pallas import tpu as pltpu

F32 = jnp.float32
BF16 = jnp.bfloat16

RMS_EPS = 1e-6
LN_EPS = 1e-5
MASK_VALUE = -1e30
HEAD_DIM = 128
CONV_HALO = 32
FFN_HALO = 16
SUBLANES = 8
LANES = 128
V7X_VMEM_LIMIT_BYTES = 56 * 1024 * 1024


def _params(n_axes, vmem_bytes=V7X_VMEM_LIMIT_BYTES):
    return pltpu.CompilerParams(dimension_semantics=("arbitrary",) * n_axes,
                                vmem_limit_bytes=vmem_bytes)


def _resident(shape):
    return pl.BlockSpec(shape, lambda *_: (0,) * len(shape), pipeline_mode=pl.Buffered(1))


def _rms_rows(x, g):
    ms = jnp.mean(x * x, axis=-1, keepdims=True)
    return x * lax.rsqrt(ms + RMS_EPS) * g


def _sigmoid(x):
    return 1.0 / (1.0 + jnp.exp(-x))


def _dot(a, b):
    return jnp.dot(a, b, preferred_element_type=F32)


def _norm_to_scratch(dst_ref, src_ref, g, rows, dst_off=0, chunk=256):
    chunk = min(chunk, rows)

    def body(c, carry):
        r = pl.multiple_of(c * chunk, chunk)
        x = src_ref[pl.ds(r, chunk), :]
        dst_ref[pl.ds(dst_off + r, chunk), :] = _rms_rows(x, g).astype(dst_ref.dtype)
        return carry

    lax.fori_loop(0, rows // chunk, body, 0)


def _norm_glu_kernel(x_ref, g_ref, wa_ref, wg_ref, ba_ref, bg_ref, o_ref, xn_ref):
    @pl.when(pl.program_id(1) == 0)
    def _():
        _norm_to_scratch(xn_ref, x_ref, g_ref[...], x_ref.shape[0])

    xn = xn_ref[...]
    a = _dot(xn, wa_ref[...]) + ba_ref[...]
    gate = _dot(xn, wg_ref[...]) + bg_ref[...]
    o_ref[...] = (a * _sigmoid(gate)).astype(o_ref.dtype)


def _column_tiles(w, tn):
    k, n = w.shape
    return w.reshape(k, n // tn, tn).transpose(1, 0, 2).astype(BF16)


def _norm_glu(x, g, w, b, tm, tn):
    s, d = x.shape
    nj = w.shape[0] // 2
    n = nj * tn
    return pl.pallas_call(
        _norm_glu_kernel,
        grid=(s // tm, nj),
        in_specs=[
            pl.BlockSpec((tm, d), lambda i, j: (i, 0)),
            pl.BlockSpec((1, d), lambda i, j: (0, 0)),
            pl.BlockSpec((None, d, tn), lambda i, j: (j, 0, 0)),
            pl.BlockSpec((None, d, tn), lambda i, j: (j + nj, 0, 0)),
            pl.BlockSpec((1, tn), lambda i, j: (0, j)),
            pl.BlockSpec((1, tn), lambda i, j: (0, j + nj)),
        ],
        out_specs=pl.BlockSpec((tm, tn), lambda i, j: (i, j)),
        out_shape=jax.ShapeDtypeStruct((s, n), F32),
        scratch_shapes=[pltpu.VMEM((tm, d), BF16)],
        compiler_params=_params(2),
        name="norm_glu",
    )(x, g, w, w, b, b)


def _norm_matmul_kernel(x_ref, g_ref, w_ref, cs_ref, o_ref, xn_ref):
    @pl.when(pl.program_id(1) == 0)
    def _():
        _norm_to_scratch(xn_ref, x_ref, g_ref[...], x_ref.shape[0])

    o_ref[...] = (_dot(xn_ref[...], w_ref[...]) * cs_ref[...]).astype(o_ref.dtype)


def _norm_matmul(x, g, w, col_scale, tm, tn, out_dtype):
    s, d = x.shape
    n = w.shape[0] * tn
    return pl.pallas_call(
        _norm_matmul_kernel,
        grid=(s // tm, n // tn),
        in_specs=[
            pl.BlockSpec((tm, d), lambda i, j: (i, 0)),
            pl.BlockSpec((1, d), lambda i, j: (0, 0)),
            pl.BlockSpec((None, d, tn), lambda i, j: (j, 0, 0)),
            pl.BlockSpec((1, tn), lambda i, j: (0, j)),
        ],
        out_specs=pl.BlockSpec((tm, tn), lambda i, j: (i, j)),
        out_shape=jax.ShapeDtypeStruct((s, n), out_dtype),
        scratch_shapes=[pltpu.VMEM((tm, d), BF16)],
        compiler_params=_params(2),
        name="norm_matmul",
    )(x, g, w, col_scale)


def _dwconv_window(win, w8_ref, lanes, rows, k):
    first = CONV_HALO - (k - 1)
    n_win = rows + CONV_HALO
    acc = None
    for b in range(SUBLANES):
        offs = [o for o in range(b, CONV_HALO + 1, SUBLANES) if 0 <= o - first < k]
        if not offs:
            continue
        shifted = win if b == 0 else pltpu.roll(win, n_win - b, axis=0)
        for o in offs:
            tap = pltpu.repeat(w8_ref[o - first, :, lanes], rows // SUBLANES, axis=0)
            term = shifted[o - b:o - b + rows] * tap
            acc = term if acc is None else acc + term
    return acc


def _conv_mix_kernel(u_ref, halo_ref, h_ref, wdw_ref, bdw_ref, lng_ref, lnb_ref, w2_ref, b2_ref,
                     pg_ref, o_ref, ext_ref, act_ref, w8_ref, *, k, rows):
    tm, d = u_ref.shape
    ext_ref[0:CONV_HALO, :] = jnp.where(pl.program_id(0) > 0, halo_ref[...], 0.0)
    ext_ref[CONV_HALO:, :] = u_ref[...]
    for j in range(k):
        w8_ref[j] = jnp.broadcast_to(wdw_ref[j:j + 1, :], (SUBLANES, d))

    n_lane = d // LANES

    def conv_body(c, carry):
        r0 = pl.multiple_of((c // n_lane) * rows, rows)
        lanes = pl.ds(pl.multiple_of((c % n_lane) * LANES, LANES), LANES)
        win = ext_ref[pl.ds(r0, rows + CONV_HALO), lanes]
        y = _dwconv_window(win, w8_ref, lanes, rows, k) + bdw_ref[:, lanes]
        ext_ref[pl.ds(r0, rows), lanes] = y
        return carry

    lax.fori_loop(0, (tm // rows) * n_lane, conv_body, 0)

    chunk = min(256, tm)

    def ln_body(c, carry):
        r = pl.multiple_of(c * chunk, chunk)
        x = ext_ref[pl.ds(r, chunk), :]
        mu = jnp.mean(x, axis=-1, keepdims=True)
        xc = x - mu
        y = xc * lax.rsqrt(jnp.mean(xc * xc, axis=-1, keepdims=True) + LN_EPS)
        y = y * lng_ref[...] + lnb_ref[...]
        act_ref[pl.ds(r, chunk), :] = (y * _sigmoid(y)).astype(act_ref.dtype)
        return carry

    lax.fori_loop(0, tm // chunk, ln_body, 0)

    y = _dot(act_ref[...], w2_ref[...]) + b2_ref[...]
    o_ref[...] = h_ref[...] + _rms_rows(y, pg_ref[...])


def _conv_mix(u, h, w_dw, b_dw, ln_g, ln_b, w2, b2, post_g, tm):
    s, d = u.shape
    k = w_dw.shape[0]
    halo_blocks = tm // CONV_HALO
    row = lambda i: (i, 0)
    return pl.pallas_call(
        functools.partial(_conv_mix_kernel, k=k, rows=min(128, tm)),
        grid=(s // tm,),
        in_specs=[
            pl.BlockSpec((tm, d), row),
            pl.BlockSpec((CONV_HALO, d), lambda i: (jnp.maximum(i * halo_blocks - 1, 0), 0)),
            pl.BlockSpec((tm, d), row),
            _resident((k, d)),
            _resident((1, d)),
            _resident((1, d)),
            _resident((1, d)),
            _resident((d, d)),
            _resident((1, d)),
            _resident((1, d)),
        ],
        out_specs=pl.BlockSpec((tm, d), row),
        out_shape=jax.ShapeDtypeStruct((s, d), F32),
        scratch_shapes=[pltpu.VMEM((tm + CONV_HALO, d), F32), pltpu.VMEM((tm, d), BF16),
                        pltpu.VMEM((k, SUBLANES, d), F32)],
        compiler_params=_params(1),
        name="conv_mix",
    )(u, u, h, w_dw, b_dw, ln_g, ln_b, w2, b2, post_g)


def _causal3(u, w, tm):
    return (u[FFN_HALO:] * w[2:3]
            + u[FFN_HALO - 1:FFN_HALO - 1 + tm] * w[1:2]
            + u[FFN_HALO - 2:FFN_HALO - 2 + tm] * w[0:1])


def _ffn_kernel(h_ref, halo_ref, g_ref, wg_ref, wu_ref, dwg_ref, dwu_ref, wo_ref, pg_ref,
                o_ref, xn_ref):
    i = pl.program_id(0)
    j = pl.program_id(1)
    tm = h_ref.shape[0]

    @pl.when(j == 0)
    def _():
        hist = jnp.where(i > 0, _rms_rows(halo_ref[...], g_ref[...]), 0.0)
        xn_ref[0:FFN_HALO, :] = hist.astype(xn_ref.dtype)
        _norm_to_scratch(xn_ref, h_ref, g_ref[...], tm, dst_off=FFN_HALO)
        o_ref[...] = jnp.zeros(o_ref.shape, o_ref.dtype)

    xn = xn_ref[...]
    gate = _causal3(_dot(xn, wg_ref[...]), dwg_ref[...], tm)
    up = _causal3(_dot(xn, wu_ref[...]), dwu_ref[...], tm)
    act = (gate * _sigmoid(gate) * up).astype(BF16)
    o_ref[...] += _dot(act, wo_ref[...])

    @pl.when(j == pl.num_programs(1) - 1)
    def _():
        chunk = min(256, tm)

        def body(c, carry):
            r = pl.multiple_of(c * chunk, chunk)
            rows = pl.ds(r, chunk)
            o_ref[rows, :] = h_ref[rows, :] + _rms_rows(o_ref[rows, :], pg_ref[...])
            return carry

        lax.fori_loop(0, tm // chunk, body, 0)


def _conv_ffn(h, pre_g, w_in, w_dw, w_out, post_g, tm, tf):
    s, d = h.shape
    f = w_out.shape[0]
    nj = f // tf
    halo_blocks = tm // FFN_HALO
    return pl.pallas_call(
        _ffn_kernel,
        grid=(s // tm, nj),
        in_specs=[
            pl.BlockSpec((tm, d), lambda i, j: (i, 0), pipeline_mode=pl.Buffered(1)),
            pl.BlockSpec((FFN_HALO, d), lambda i, j: (jnp.maximum(i * halo_blocks - 1, 0), 0)),
            pl.BlockSpec((1, d), lambda i, j: (0, 0)),
            pl.BlockSpec((None, d, tf), lambda i, j: (j, 0, 0)),
            pl.BlockSpec((None, d, tf), lambda i, j: (j + nj, 0, 0)),
            pl.BlockSpec((3, tf), lambda i, j: (0, j)),
            pl.BlockSpec((3, tf), lambda i, j: (0, j + nj)),
            pl.BlockSpec((tf, d), lambda i, j: (j, 0)),
            pl.BlockSpec((1, d), lambda i, j: (0, 0)),
        ],
        out_specs=pl.BlockSpec((tm, d), lambda i, j: (i, 0)),
        out_shape=jax.ShapeDtypeStruct((s, d), F32),
        scratch_shapes=[pltpu.VMEM((tm + FFN_HALO, d), BF16)],
        compiler_params=_params(2),
        name="conv_ffn",
    )(h, h, pre_g, w_in, w_in, w_dw, w_dw, w_out, post_g)


def _ple_kernel(h_ref, p_ref, g_ref, wg_ref, bg_ref, wp_ref, pg_ref, o_ref):
    h = h_ref[...]
    xn = _rms_rows(h, g_ref[...]).astype(BF16)
    gate = _sigmoid(_dot(xn, wg_ref[...]) + bg_ref[...])
    emb = _dot(p_ref[...].astype(BF16), wp_ref[...])
    o_ref[...] = h + _rms_rows(gate * emb, pg_ref[...])


def _per_layer_embed(h, p, p_index, pre_g, w_gate, b_gate, w_proj, post_g, tm):
    s, d = h.shape
    e = p.shape[2]
    row = lambda i: (i, 0)
    return pl.pallas_call(
        _ple_kernel,
        grid=(s // tm,),
        in_specs=[
            pl.BlockSpec((tm, d), row),
            pl.BlockSpec((None, tm, e), lambda i: (p_index, i, 0)),
            _resident((1, d)),
            _resident((d, d)),
            _resident((1, d)),
            _resident((e, d)),
            _resident((1, d)),
        ],
        out_specs=pl.BlockSpec((tm, d), row),
        out_shape=jax.ShapeDtypeStruct((s, d), F32),
        compiler_params=_params(1),
        name="per_layer_embed",
    )(h, p, pre_g, w_gate, b_gate, w_proj, post_g)


L_FLOOR = 2.0 ** -64
TILES_PER_TRIP = 4


def _diff_attn_kernel(q_ref, k_ref, v_ref, lq1_ref, lk1_ref, lq2_ref, lk2_ref, sg_ref, o_ref,
                      b_ref, l_ref, acc_ref, kn_ref, *, lambda_init):
    qi = pl.program_id(1)
    tq = q_ref.shape[0]
    tk = tq
    n_lane_tiles = tk // LANES
    n_keys = k_ref.shape[0]

    def map_lanes(c):
        return slice(c * HEAD_DIM, (c + 1) * HEAD_DIM)

    def scores(k0, c, diagonal):
        s = lax.dot_general(q_ref[:, map_lanes(c)], k_ref[pl.ds(k0, tk), map_lanes(c)],
                            (((1,), (1,)), ((), ())), preferred_element_type=F32)
        if diagonal:
            rows = lax.broadcasted_iota(jnp.int32, s.shape, 0)
            cols = lax.broadcasted_iota(jnp.int32, s.shape, 1)
            s = jnp.where(rows >= cols, s, MASK_VALUE)
        return s

    @pl.when(qi == 0)
    def _():
        chunk = min(512, n_keys)
        for c in range(2):
            def body(i, best):
                rows = pl.ds(pl.multiple_of(i * chunk, chunk), chunk)
                kk = k_ref[rows, map_lanes(c)].astype(F32)
                return jnp.maximum(best, jnp.sum(kk * kk, axis=1, keepdims=True))

            best = lax.fori_loop(0, n_keys // chunk, body, jnp.zeros((chunk, 1), F32))
            kn_ref[c] = jnp.broadcast_to(jnp.sqrt(jnp.max(best, axis=0, keepdims=True)),
                                         (SUBLANES, LANES))

    def tile(k0, diagonal):
        v = v_ref[pl.ds(k0, tk), :]
        for c in range(2):
            p = jnp.exp2(scores(k0, c, diagonal) - pltpu.repeat(b_ref[c], n_lane_tiles, axis=1))
            psum = p[:, 0:LANES]
            for t in range(1, n_lane_tiles):
                psum = psum + p[:, t * LANES:(t + 1) * LANES]
            l_ref[c] += psum
            acc_ref[c] += _dot(p.astype(BF16), v)

    def start(t):
        return pl.multiple_of(t * tk, tk)

    def sweep():
        l_ref[...] = jnp.zeros(l_ref.shape, F32)
        acc_ref[...] = jnp.zeros(acc_ref.shape, F32)

        def trip(g, carry):
            for u in range(TILES_PER_TRIP):
                tile(start(TILES_PER_TRIP * g + u), False)
            return carry

        lax.fori_loop(0, qi // TILES_PER_TRIP, trip, 0)
        base = (qi // TILES_PER_TRIP) * TILES_PER_TRIP
        rem = qi % TILES_PER_TRIP

        @pl.when(rem >= 2)
        def _():
            tile(start(base), False)
            tile(start(base + 1), False)

        last = base + jnp.where(rem >= 2, 2, 0)

        @pl.when(rem % 2 == 1)
        def _():
            tile(start(last), False)
            tile(start(qi), True)

        @pl.when(rem % 2 == 0)
        def _():
            tile(start(qi), True)

    def exact_row_max():
        for c in range(2):
            def fold(best, s):
                for u in range(n_lane_tiles):
                    best = jnp.maximum(best, s[:, u * LANES:(u + 1) * LANES])
                return best

            best = lax.fori_loop(0, qi, lambda t, best: fold(best, scores(start(t), c, False)),
                                 jnp.full((tq, LANES), MASK_VALUE, F32))
            best = fold(best, scores(start(qi), c, True))
            b_ref[c] = jnp.broadcast_to(jnp.max(best, axis=1, keepdims=True), (tq, LANES))

    for c in range(2):
        qq = q_ref[:, map_lanes(c)].astype(F32)
        b_ref[c] = jnp.sqrt(jnp.sum(qq * qq, axis=1, keepdims=True)) * kn_ref[c, 0:1, :]
    sweep()

    l_min = jnp.min(jnp.minimum(jnp.sum(l_ref[0], axis=1, keepdims=True),
                                jnp.sum(l_ref[1], axis=1, keepdims=True)))

    @pl.when(jnp.logical_not(l_min >= L_FLOOR))
    def _():
        exact_row_max()
        sweep()

    lam = (jnp.exp(jnp.sum(lq1_ref[...] * lk1_ref[...], keepdims=True))
           - jnp.exp(jnp.sum(lq2_ref[...] * lk2_ref[...], keepdims=True)) + lambda_init)
    l0 = jnp.sum(l_ref[0], axis=1, keepdims=True)
    l1 = jnp.sum(l_ref[1], axis=1, keepdims=True)
    o = acc_ref[0] / l0 - lam * (acc_ref[1] / l1)
    o_ref[...] = (_rms_rows(o, sg_ref[...]) * (1.0 - lambda_init)).astype(o_ref.dtype)


def _diff_attention(qkv, lq1, lk1, lq2, lk2, subln_g, lambda_init, tq):
    s = qkv.shape[0]
    vd = 2 * HEAD_DIM
    heads = qkv.shape[1] // (3 * vd)
    vec = pl.BlockSpec((1, HEAD_DIM), lambda h, i: (0, 0))
    return pl.pallas_call(
        functools.partial(_diff_attn_kernel, lambda_init=lambda_init),
        grid=(heads, s // tq),
        in_specs=[
            pl.BlockSpec((tq, vd), lambda h, i: (i, h)),
            pl.BlockSpec((s, vd), lambda h, i: (0, heads + h)),
            pl.BlockSpec((s, vd), lambda h, i: (0, 2 * heads + h)),
            vec, vec, vec, vec,
            pl.BlockSpec((1, vd), lambda h, i: (0, 0)),
        ],
        out_specs=pl.BlockSpec((tq, vd), lambda h, i: (i, h)),
        out_shape=jax.ShapeDtypeStruct((s, heads * vd), BF16),
        scratch_shapes=[pltpu.VMEM((2, tq, LANES), F32), pltpu.VMEM((2, tq, LANES), F32),
                        pltpu.VMEM((2, tq, vd), F32), pltpu.VMEM((2, SUBLANES, LANES), F32)],
        compiler_params=_params(2),
        name="diff_attention",
    )(qkv, qkv, qkv, lq1, lk1, lq2, lk2, subln_g)


def _proj_residual_kernel(a_ref, h_ref, w_ref, pg_ref, o_ref):
    o_ref[...] = h_ref[...] + _rms_rows(_dot(a_ref[...], w_ref[...]), pg_ref[...])


def _proj_residual(a, h, w, post_g, tm):
    s, d = h.shape
    row = lambda i: (i, 0)
    return pl.pallas_call(
        _proj_residual_kernel,
        grid=(s // tm,),
        in_specs=[pl.BlockSpec((tm, a.shape[1]), row), pl.BlockSpec((tm, d), row),
                  _resident(w.shape), _resident((1, d))],
        out_specs=pl.BlockSpec((tm, d), row),
        out_shape=jax.ShapeDtypeStruct((s, d), F32),
        compiler_params=_params(1),
        name="proj_residual",
    )(a, h, w, post_g)


def _tiles(s, d, f):
    ffn_cols = 512 if f % 512 == 0 else 256
    return dict(
        proj_rows=min(1024, s), proj_cols=min(512, d),
        conv_rows=min(256, s),
        ffn_rows=min(1024, s), ffn_cols=min(ffn_cols, f),
        ple_rows=min(512, s),
        attn_rows=min(512, s),
        out_rows=min(512, s),
    )


def kernel(x, p, pre_mix_g, post_mix_g, conv_w_pw1, conv_b_pw1, conv_w_dw, conv_b_dw, conv_ln_g, conv_ln_b, conv_w_pw2, conv_b_pw2, attn_w_qkv, attn_lambda_q1, attn_lambda_k1, attn_lambda_q2, attn_lambda_k2, attn_subln_g, attn_w_o, pre_ffn_g, post_ffn_g, ffn_w_in, ffn_w_dw, ffn_w_out, ple_pre_g, ple_w_gate, ple_b_gate, ple_w_proj, ple_post_g):
    batch, s, d = x.shape
    depth = p.shape[0]
    f = ffn_w_out.shape[1]
    t = _tiles(s, d, f)
    vec = lambda a: a.reshape(1, -1)
    bf = lambda a: a.astype(BF16)
    p_all = p.reshape(depth * batch, s, p.shape[-1])

    outs = []
    for b in range(batch):
        h = x.reshape(s, d) if batch == 1 else x[b]
        for i in range(depth):
            j = i // 2
            if i % 2 == 0:
                u = _norm_glu(h, vec(pre_mix_g[i]), _column_tiles(conv_w_pw1[j], t["proj_cols"]),
                              vec(conv_b_pw1[j]), t["proj_rows"], t["proj_cols"])
                h = _conv_mix(u, h, conv_w_dw[j], vec(conv_b_dw[j]), vec(conv_ln_g[j]),
                              vec(conv_ln_b[j]), bf(conv_w_pw2[j]), vec(conv_b_pw2[j]),
                              vec(post_mix_g[i]), t["conv_rows"])
            else:
                q_scale = HEAD_DIM ** -0.5 * math.log2(math.e)
                col_scale = jnp.concatenate([jnp.full((1, d), q_scale, F32),
                                             jnp.ones((1, 2 * d), F32)], axis=1)
                qkv = _norm_matmul(h, vec(pre_mix_g[i]),
                                   _column_tiles(attn_w_qkv[j], t["proj_cols"]), col_scale,
                                   t["proj_rows"], t["proj_cols"], BF16)
                lambda_init = 0.8 - 0.6 * math.exp(-0.3 * i)
                o = _diff_attention(qkv, vec(attn_lambda_q1[j]), vec(attn_lambda_k1[j]),
                                    vec(attn_lambda_q2[j]), vec(attn_lambda_k2[j]),
                                    vec(attn_subln_g[j]), lambda_init, t["attn_rows"])
                h = _proj_residual(o, h, bf(attn_w_o[j]), vec(post_mix_g[i]), t["out_rows"])
            h = _conv_ffn(h, vec(pre_ffn_g[i]), _column_tiles(ffn_w_in[i], t["ffn_cols"]),
                          ffn_w_dw[i], bf(ffn_w_out[i]), vec(post_ffn_g[i]),
                          t["ffn_rows"], t["ffn_cols"])
            h = _per_layer_embed(h, p_all, i * batch + b, vec(ple_pre_g[i]), bf(ple_w_gate[i]),
                                 vec(ple_b_gate[i]), bf(ple_w_proj[i]), vec(ple_post_g[i]),
                                 t["ple_rows"])
        outs.append(h)
    return outs[0].reshape(1, s, d) if batch == 1 else jnp.stack(outs)
```

```python
import functools
import math

import jax
import jax.numpy as jnp
from jax import lax
from jax.experimental import pallas as pl
from jax.experimental.pallas import tpu as pltpu

F32 = jnp.float32
BF16 = jnp.bfloat16

RMS_EPS = 1e-6
LN_EPS = 1e-5
MASK_VALUE = -1e30
HEAD_DIM = 128
CONV_HALO = 32
FFN_HALO = 16
SUBLANES = 8
LANES = 128
V7X_VMEM_LIMIT_BYTES = 56 * 1024 * 1024


def _params(n_axes, vmem_bytes=V7X_VMEM_LIMIT_BYTES):
    return pltpu.CompilerParams(dimension_semantics=("arbitrary",) * n_axes,
                                vmem_limit_bytes=vmem_bytes)


def _resident(shape):
    return pl.BlockSpec(shape, lambda *_: (0,) * len(shape), pipeline_mode=pl.Buffered(1))


def _rms_rows(x, g):
    ms = jnp.mean(x * x, axis=-1, keepdims=True)
    return x * lax.rsqrt(ms + RMS_EPS) * g


def _sigmoid(x):
    return 1.0 / (1.0 + jnp.exp(-x))


def _dot(a, b):
    return jnp.dot(a, b, preferred_element_type=F32)


def _norm_to_scratch(dst_ref, src_ref, g, rows, dst_off=0, chunk=256):
    chunk = min(chunk, rows)

    def body(c, carry):
        r = pl.multiple_of(c * chunk, chunk)
        x = src_ref[pl.ds(r, chunk), :]
        dst_ref[pl.ds(dst_off + r, chunk), :] = _rms_rows(x, g).astype(dst_ref.dtype)
        return carry

    lax.fori_loop(0, rows // chunk, body, 0)


def _norm_glu_kernel(x_ref, g_ref, wa_ref, wg_ref, ba_ref, bg_ref, o_ref, xn_ref):
    @pl.when(pl.program_id(1) == 0)
    def _():
        _norm_to_scratch(xn_ref, x_ref, g_ref[...], x_ref.shape[0])

    xn = xn_ref[...]
    a = _dot(xn, wa_ref[...]) + ba_ref[...]
    gate = _dot(xn, wg_ref[...]) + bg_ref[...]
    o_ref[...] = (a * _sigmoid(gate)).astype(o_ref.dtype)


def _norm_glu(x, g, w, b, tm, tn):
    s, d = x.shape
    n = w.shape[1] // 2
    nj = n // tn
    return pl.pallas_call(
        _norm_glu_kernel,
        grid=(s // tm, nj),
        in_specs=[
            pl.BlockSpec((tm, d), lambda i, j: (i, 0)),
            pl.BlockSpec((1, d), lambda i, j: (0, 0)),
            pl.BlockSpec((d, tn), lambda i, j: (0, j)),
            pl.BlockSpec((d, tn), lambda i, j: (0, j + nj)),
            pl.BlockSpec((1, tn), lambda i, j: (0, j)),
            pl.BlockSpec((1, tn), lambda i, j: (0, j + nj)),
        ],
        out_specs=pl.BlockSpec((tm, tn), lambda i, j: (i, j)),
        out_shape=jax.ShapeDtypeStruct((s, n), F32),
        scratch_shapes=[pltpu.VMEM((tm, d), BF16)],
        compiler_params=_params(2),
        name="norm_glu",
    )(x, g, w, w, b, b)


def _norm_matmul_kernel(x_ref, g_ref, w_ref, cs_ref, o_ref, xn_ref):
    @pl.when(pl.program_id(1) == 0)
    def _():
        _norm_to_scratch(xn_ref, x_ref, g_ref[...], x_ref.shape[0])

    o_ref[...] = (_dot(xn_ref[...], w_ref[...]) * cs_ref[...]).astype(o_ref.dtype)


def _norm_matmul(x, g, w, col_scale, tm, tn, out_dtype):
    s, d = x.shape
    n = w.shape[1]
    return pl.pallas_call(
        _norm_matmul_kernel,
        grid=(s // tm, n // tn),
        in_specs=[
            pl.BlockSpec((tm, d), lambda i, j: (i, 0)),
            pl.BlockSpec((1, d), lambda i, j: (0, 0)),
            pl.BlockSpec((d, tn), lambda i, j: (0, j)),
            pl.BlockSpec((1, tn), lambda i, j: (0, j)),
        ],
        out_specs=pl.BlockSpec((tm, tn), lambda i, j: (i, j)),
        out_shape=jax.ShapeDtypeStruct((s, n), out_dtype),
        scratch_shapes=[pltpu.VMEM((tm, d), BF16)],
        compiler_params=_params(2),
        name="norm_matmul",
    )(x, g, w, col_scale)


def _dwconv_window(win, w8_ref, lanes, rows, k):
    first = CONV_HALO - (k - 1)
    n_win = rows + CONV_HALO
    acc = None
    for b in range(SUBLANES):
        offs = [o for o in range(b, CONV_HALO + 1, SUBLANES) if 0 <= o - first < k]
        if not offs:
            continue
        shifted = win if b == 0 else pltpu.roll(win, n_win - b, axis=0)
        for o in offs:
            tap = pltpu.repeat(w8_ref[o - first, :, lanes], rows // SUBLANES, axis=0)
            term = shifted[o - b:o - b + rows] * tap
            acc = term if acc is None else acc + term
    return acc


def _conv_mix_kernel(u_ref, halo_ref, h_ref, wdw_ref, bdw_ref, lng_ref, lnb_ref, w2_ref, b2_ref,
                     pg_ref, o_ref, ext_ref, act_ref, w8_ref, *, k, rows):
    tm, d = u_ref.shape
    ext_ref[0:CONV_HALO, :] = jnp.where(pl.program_id(0) > 0, halo_ref[...], 0.0)
    ext_ref[CONV_HALO:, :] = u_ref[...]
    for j in range(k):
        w8_ref[j] = jnp.broadcast_to(wdw_ref[j:j + 1, :], (SUBLANES, d))

    n_lane = d // LANES

    def conv_body(c, carry):
        r0 = pl.multiple_of((c // n_lane) * rows, rows)
        lanes = pl.ds(pl.multiple_of((c % n_lane) * LANES, LANES), LANES)
        win = ext_ref[pl.ds(r0, rows + CONV_HALO), lanes]
        y = _dwconv_window(win, w8_ref, lanes, rows, k) + bdw_ref[:, lanes]
        ext_ref[pl.ds(r0, rows), lanes] = y
        return carry

    lax.fori_loop(0, (tm // rows) * n_lane, conv_body, 0)

    chunk = min(256, tm)

    def ln_body(c, carry):
        r = pl.multiple_of(c * chunk, chunk)
        x = ext_ref[pl.ds(r, chunk), :]
        mu = jnp.mean(x, axis=-1, keepdims=True)
        xc = x - mu
        y = xc * lax.rsqrt(jnp.mean(xc * xc, axis=-1, keepdims=True) + LN_EPS)
        y = y * lng_ref[...] + lnb_ref[...]
        act_ref[pl.ds(r, chunk), :] = (y * _sigmoid(y)).astype(act_ref.dtype)
        return carry

    lax.fori_loop(0, tm // chunk, ln_body, 0)

    y = _dot(act_ref[...], w2_ref[...]) + b2_ref[...]
    o_ref[...] = h_ref[...] + _rms_rows(y, pg_ref[...])


def _conv_mix(u, h, w_dw, b_dw, ln_g, ln_b, w2, b2, post_g, tm):
    s, d = u.shape
    k = w_dw.shape[0]
    halo_blocks = tm // CONV_HALO
    row = lambda i: (i, 0)
    return pl.pallas_call(
        functools.partial(_conv_mix_kernel, k=k, rows=min(128, tm)),
        grid=(s // tm,),
        in_specs=[
            pl.BlockSpec((tm, d), row),
            pl.BlockSpec((CONV_HALO, d), lambda i: (jnp.maximum(i * halo_blocks - 1, 0), 0)),
            pl.BlockSpec((tm, d), row),
            _resident((k, d)),
            _resident((1, d)),
            _resident((1, d)),
            _resident((1, d)),
            _resident((d, d)),
            _resident((1, d)),
            _resident((1, d)),
        ],
        out_specs=pl.BlockSpec((tm, d), row),
        out_shape=jax.ShapeDtypeStruct((s, d), F32),
        scratch_shapes=[pltpu.VMEM((tm + CONV_HALO, d), F32), pltpu.VMEM((tm, d), BF16),
                        pltpu.VMEM((k, SUBLANES, d), F32)],
        compiler_params=_params(1),
        name="conv_mix",
    )(u, u, h, w_dw, b_dw, ln_g, ln_b, w2, b2, post_g)


def _causal3(u, w, tm):
    return (u[FFN_HALO:] * w[2:3]
            + u[FFN_HALO - 1:FFN_HALO - 1 + tm] * w[1:2]
            + u[FFN_HALO - 2:FFN_HALO - 2 + tm] * w[0:1])


def _ffn_kernel(h_ref, halo_ref, g_ref, wg_ref, wu_ref, dwg_ref, dwu_ref, wo_ref, pg_ref,
                o_ref, xn_ref):
    i = pl.program_id(0)
    j = pl.program_id(1)
    tm = h_ref.shape[0]

    @pl.when(j == 0)
    def _():
        hist = jnp.where(i > 0, _rms_rows(halo_ref[...], g_ref[...]), 0.0)
        xn_ref[0:FFN_HALO, :] = hist.astype(xn_ref.dtype)
        _norm_to_scratch(xn_ref, h_ref, g_ref[...], tm, dst_off=FFN_HALO)
        o_ref[...] = jnp.zeros(o_ref.shape, o_ref.dtype)

    xn = xn_ref[...]
    gate = _causal3(_dot(xn, wg_ref[...]), dwg_ref[...], tm)
    up = _causal3(_dot(xn, wu_ref[...]), dwu_ref[...], tm)
    act = (gate * _sigmoid(gate) * up).astype(BF16)
    o_ref[...] += _dot(act, wo_ref[...])

    @pl.when(j == pl.num_programs(1) - 1)
    def _():
        chunk = min(256, tm)

        def body(c, carry):
            r = pl.multiple_of(c * chunk, chunk)
            rows = pl.ds(r, chunk)
            o_ref[rows, :] = h_ref[rows, :] + _rms_rows(o_ref[rows, :], pg_ref[...])
            return carry

        lax.fori_loop(0, tm // chunk, body, 0)


def _conv_ffn(h, pre_g, w_in, w_dw, w_out, layer, post_g, tm, tf):
    s, d = h.shape
    f = w_out.shape[1]
    nj = f // tf
    halo_blocks = tm // FFN_HALO
    return pl.pallas_call(
        _ffn_kernel,
        grid=(s // tm, nj),
        in_specs=[
            pl.BlockSpec((tm, d), lambda i, j: (i, 0), pipeline_mode=pl.Buffered(1)),
            pl.BlockSpec((FFN_HALO, d), lambda i, j: (jnp.maximum(i * halo_blocks - 1, 0), 0)),
            pl.BlockSpec((1, d), lambda i, j: (0, 0)),
            pl.BlockSpec((None, d, tf), lambda i, j: (layer, 0, j)),
            pl.BlockSpec((None, d, tf), lambda i, j: (layer, 0, j + nj)),
            pl.BlockSpec((None, 3, tf), lambda i, j: (layer, 0, j)),
            pl.BlockSpec((None, 3, tf), lambda i, j: (layer, 0, j + nj)),
            pl.BlockSpec((None, tf, d), lambda i, j: (layer, j, 0)),
            pl.BlockSpec((1, d), lambda i, j: (0, 0)),
        ],
        out_specs=pl.BlockSpec((tm, d), lambda i, j: (i, 0)),
        out_shape=jax.ShapeDtypeStruct((s, d), F32),
        scratch_shapes=[pltpu.VMEM((tm + FFN_HALO, d), BF16)],
        compiler_params=_params(2),
        name="conv_ffn",
    )(h, h, pre_g, w_in, w_in, w_dw, w_dw, w_out, post_g)


def _ple_kernel(h_ref, p_ref, g_ref, wg_ref, bg_ref, wp_ref, pg_ref, o_ref):
    h = h_ref[...]
    xn = _rms_rows(h, g_ref[...]).astype(BF16)
    gate = _sigmoid(_dot(xn, wg_ref[...]) + bg_ref[...])
    emb = _dot(p_ref[...].astype(BF16), wp_ref[...])
    o_ref[...] = h + _rms_rows(gate * emb, pg_ref[...])


def _per_layer_embed(h, p, p_index, pre_g, w_gate, b_gate, w_proj, layer, post_g, tm):
    s, d = h.shape
    e = p.shape[2]
    row = lambda i: (i, 0)
    layer_resident = lambda shape: pl.BlockSpec((None,) + shape, lambda i: (layer, 0, 0),
                                                pipeline_mode=pl.Buffered(1))
    return pl.pallas_call(
        _ple_kernel,
        grid=(s // tm,),
        in_specs=[
            pl.BlockSpec((tm, d), row),
            pl.BlockSpec((None, tm, e), lambda i: (p_index, i, 0)),
            _resident((1, d)),
            layer_resident((d, d)),
            _resident((1, d)),
            layer_resident((e, d)),
            _resident((1, d)),
        ],
        out_specs=pl.BlockSpec((tm, d), row),
        out_shape=jax.ShapeDtypeStruct((s, d), F32),
        compiler_params=_params(1),
        name="per_layer_embed",
    )(h, p, pre_g, w_gate, b_gate, w_proj, post_g)


L_FLOOR = 2.0 ** -64
TILES_PER_TRIP = 4


def _diff_attn_kernel(q_ref, k_ref, v_ref, lq1_ref, lk1_ref, lq2_ref, lk2_ref, sg_ref, o_ref,
                      b_ref, l_ref, acc_ref, kn_ref, *, lambda_init):
    qi = pl.program_id(1)
    tq = q_ref.shape[0]
    tk = tq
    n_lane_tiles = tk // LANES
    n_keys = k_ref.shape[0]

    def map_lanes(c):
        return slice(c * HEAD_DIM, (c + 1) * HEAD_DIM)

    def scores(k0, c, diagonal):
        s = lax.dot_general(q_ref[:, map_lanes(c)], k_ref[pl.ds(k0, tk), map_lanes(c)],
                            (((1,), (1,)), ((), ())), preferred_element_type=F32)
        if diagonal:
            rows = lax.broadcasted_iota(jnp.int32, s.shape, 0)
            cols = lax.broadcasted_iota(jnp.int32, s.shape, 1)
            s = jnp.where(rows >= cols, s, MASK_VALUE)
        return s

    @pl.when(qi == 0)
    def _():
        chunk = min(512, n_keys)
        for c in range(2):
            def body(i, best):
                rows = pl.ds(pl.multiple_of(i * chunk, chunk), chunk)
                kk = k_ref[rows, map_lanes(c)].astype(F32)
                return jnp.maximum(best, jnp.sum(kk * kk, axis=1, keepdims=True))

            best = lax.fori_loop(0, n_keys // chunk, body, jnp.zeros((chunk, 1), F32))
            kn_ref[c] = jnp.broadcast_to(jnp.sqrt(jnp.max(best, axis=0, keepdims=True)),
                                         (SUBLANES, LANES))

    def tile(k0, diagonal):
        v = v_ref[pl.ds(k0, tk), :]
        for c in range(2):
            p = jnp.exp2(scores(k0, c, diagonal) - pltpu.repeat(b_ref[c], n_lane_tiles, axis=1))
            psum = p[:, 0:LANES]
            for t in range(1, n_lane_tiles):
                psum = psum + p[:, t * LANES:(t + 1) * LANES]
            l_ref[c] += psum
            acc_ref[c] += _dot(p.astype(BF16), v)

    def start(t):
        return pl.multiple_of(t * tk, tk)

    def sweep():
        l_ref[...] = jnp.zeros(l_ref.shape, F32)
        acc_ref[...] = jnp.zeros(acc_ref.shape, F32)

        def trip(g, carry):
            for u in range(TILES_PER_TRIP):
                tile(start(TILES_PER_TRIP * g + u), False)
            return carry

        lax.fori_loop(0, qi // TILES_PER_TRIP, trip, 0)
        base = (qi // TILES_PER_TRIP) * TILES_PER_TRIP
        rem = qi % TILES_PER_TRIP

        @pl.when(rem >= 2)
        def _():
            tile(start(base), False)
            tile(start(base + 1), False)

        last = base + jnp.where(rem >= 2, 2, 0)

        @pl.when(rem % 2 == 1)
        def _():
            tile(start(last), False)
            tile(start(qi), True)

        @pl.when(rem % 2 == 0)
        def _():
            tile(start(qi), True)

    def exact_row_max():
        for c in range(2):
            def fold(best, s):
                for u in range(n_lane_tiles):
                    best = jnp.maximum(best, s[:, u * LANES:(u + 1) * LANES])
                return best

            best = lax.fori_loop(0, qi, lambda t, best: fold(best, scores(start(t), c, False)),
                                 jnp.full((tq, LANES), MASK_VALUE, F32))
            best = fold(best, scores(start(qi), c, True))
            b_ref[c] = jnp.broadcast_to(jnp.max(best, axis=1, keepdims=True), (tq, LANES))

    for c in range(2):
        qq = q_ref[:, map_lanes(c)].astype(F32)
        b_ref[c] = jnp.sqrt(jnp.sum(qq * qq, axis=1, keepdims=True)) * kn_ref[c, 0:1, :]
    sweep()

    l_min = jnp.min(jnp.minimum(jnp.sum(l_ref[0], axis=1, keepdims=True),
                                jnp.sum(l_ref[1], axis=1, keepdims=True)))

    @pl.when(jnp.logical_not(l_min >= L_FLOOR))
    def _():
        exact_row_max()
        sweep()

    lam = (jnp.exp(jnp.sum(lq1_ref[...] * lk1_ref[...], keepdims=True))
           - jnp.exp(jnp.sum(lq2_ref[...] * lk2_ref[...], keepdims=True)) + lambda_init)
    l0 = jnp.sum(l_ref[0], axis=1, keepdims=True)
    l1 = jnp.sum(l_ref[1], axis=1, keepdims=True)
    o = acc_ref[0] / l0 - lam * (acc_ref[1] / l1)
    o_ref[...] = (_rms_rows(o, sg_ref[...]) * (1.0 - lambda_init)).astype(o_ref.dtype)


def _diff_attention(qkv, lq1, lk1, lq2, lk2, subln_g, lambda_init, tq):
    s = qkv.shape[0]
    vd = 2 * HEAD_DIM
    heads = qkv.shape[1] // (3 * vd)
    vec = pl.BlockSpec((1, HEAD_DIM), lambda h, i: (0, 0))
    return pl.pallas_call(
        functools.partial(_diff_attn_kernel, lambda_init=lambda_init),
        grid=(heads, s // tq),
        in_specs=[
            pl.BlockSpec((tq, vd), lambda h, i: (i, h)),
            pl.BlockSpec((s, vd), lambda h, i: (0, heads + h)),
            pl.BlockSpec((s, vd), lambda h, i: (0, 2 * heads + h)),
            vec, vec, vec, vec,
            pl.BlockSpec((1, vd), lambda h, i: (0, 0)),
        ],
        out_specs=pl.BlockSpec((tq, vd), lambda h, i: (i, h)),
        out_shape=jax.ShapeDtypeStruct((s, heads * vd), BF16),
        scratch_shapes=[pltpu.VMEM((2, tq, LANES), F32), pltpu.VMEM((2, tq, LANES), F32),
                        pltpu.VMEM((2, tq, vd), F32), pltpu.VMEM((2, SUBLANES, LANES), F32)],
        compiler_params=_params(2),
        name="diff_attention",
    )(qkv, qkv, qkv, lq1, lk1, lq2, lk2, subln_g)


def _proj_residual_kernel(a_ref, h_ref, w_ref, pg_ref, o_ref):
    o_ref[...] = h_ref[...] + _rms_rows(_dot(a_ref[...], w_ref[...]), pg_ref[...])


def _proj_residual(a, h, w, post_g, tm):
    s, d = h.shape
    row = lambda i: (i, 0)
    return pl.pallas_call(
        _proj_residual_kernel,
        grid=(s // tm,),
        in_specs=[pl.BlockSpec((tm, a.shape[1]), row), pl.BlockSpec((tm, d), row),
                  _resident(w.shape), _resident((1, d))],
        out_specs=pl.BlockSpec((tm, d), row),
        out_shape=jax.ShapeDtypeStruct((s, d), F32),
        compiler_params=_params(1),
        name="proj_residual",
    )(a, h, w, post_g)


def _tiles(s, d, f):
    ffn_cols = 512 if f % 512 == 0 else 256
    return dict(
        proj_rows=min(1024, s), proj_cols=min(512, d), qkv_cols=min(1024, d),
        conv_rows=min(512, s),
        ffn_rows=min(1024, s), ffn_cols=min(ffn_cols, f),
        ple_rows=min(512, s),
        attn_rows=min(512, s),
        out_rows=min(512, s),
    )


def kernel(x, p, pre_mix_g, post_mix_g, conv_w_pw1, conv_b_pw1, conv_w_dw, conv_b_dw, conv_ln_g, conv_ln_b, conv_w_pw2, conv_b_pw2, attn_w_qkv, attn_lambda_q1, attn_lambda_k1, attn_lambda_q2, attn_lambda_k2, attn_subln_g, attn_w_o, pre_ffn_g, post_ffn_g, ffn_w_in, ffn_w_dw, ffn_w_out, ple_pre_g, ple_w_gate, ple_b_gate, ple_w_proj, ple_post_g):
    batch, s, d = x.shape
    depth = p.shape[0]
    f = ffn_w_out.shape[1]
    t = _tiles(s, d, f)
    vec = lambda a: a.reshape(1, -1)
    bf = lambda a: a.astype(BF16)
    p_all = p.reshape(depth * batch, s, p.shape[-1])
    ffn_w_in_bf, ffn_w_out_bf = bf(ffn_w_in), bf(ffn_w_out)
    ple_w_gate_bf, ple_w_proj_bf = bf(ple_w_gate), bf(ple_w_proj)

    outs = []
    for b in range(batch):
        h = x.reshape(s, d) if batch == 1 else x[b]
        for i in range(depth):
            j = i // 2
            if i % 2 == 0:
                u = _norm_glu(h, vec(pre_mix_g[i]), bf(conv_w_pw1[j]), vec(conv_b_pw1[j]),
                              t["proj_rows"], t["proj_cols"])
                h = _conv_mix(u, h, conv_w_dw[j], vec(conv_b_dw[j]), vec(conv_ln_g[j]),
                              vec(conv_ln_b[j]), bf(conv_w_pw2[j]), vec(conv_b_pw2[j]),
                              vec(post_mix_g[i]), t["conv_rows"])
            else:
                q_scale = HEAD_DIM ** -0.5 * math.log2(math.e)
                col_scale = jnp.concatenate([jnp.full((1, d), q_scale, F32),
                                             jnp.ones((1, 2 * d), F32)], axis=1)
                qkv = _norm_matmul(h, vec(pre_mix_g[i]), bf(attn_w_qkv[j]), col_scale,
                                   t["proj_rows"], t["qkv_cols"], BF16)
                lambda_init = 0.8 - 0.6 * math.exp(-0.3 * i)
                o = _diff_attention(qkv, vec(attn_lambda_q1[j]), vec(attn_lambda_k1[j]),
                                    vec(attn_lambda_q2[j]), vec(attn_lambda_k2[j]),
                                    vec(attn_subln_g[j]), lambda_init, t["attn_rows"])
                h = _proj_residual(o, h, bf(attn_w_o[j]), vec(post_mix_g[i]), t["out_rows"])
            h = _conv_ffn(h, vec(pre_ffn_g[i]), ffn_w_in_bf, ffn_w_dw, ffn_w_out_bf, i,
                          vec(post_ffn_g[i]), t["ffn_rows"], t["ffn_cols"])
            h = _per_layer_embed(h, p_all, i * batch + b, vec(ple_pre_g[i]), ple_w_gate_bf,
                                 vec(ple_b_gate[i]), ple_w_proj_bf, i, vec(ple_post_g[i]),
                                 t["ple_rows"])
        outs.append(h)
    return outs[0].reshape(1, s, d) if batch == 1 else jnp.stack(outs)
```

```python
import functools
import math

import jax
import jax.numpy as jnp
from jax import lax
from jax.experimental import pallas as pl
from jax.experimental.pallas import tpu as pltpu

F32 = jnp.float32
BF16 = jnp.bfloat16

RMS_EPS = 1e-6
LN_EPS = 1e-5
MASK_VALUE = -1e30
HEAD_DIM = 128
CONV_HALO = 32
FFN_HALO = 16
SUBLANES = 8
LANES = 128
V7X_VMEM_LIMIT_BYTES = 56 * 1024 * 1024


def _params(n_axes, vmem_bytes=V7X_VMEM_LIMIT_BYTES):
    return pltpu.CompilerParams(dimension_semantics=("arbitrary",) * n_axes,
                                vmem_limit_bytes=vmem_bytes)


def _resident(shape):
    return pl.BlockSpec(shape, lambda *_: (0,) * len(shape), pipeline_mode=pl.Buffered(1))


def _cast_sidecar(w, layer, n_steps, step_of):
    rows, cols = w.shape[1:]
    bf16_rows = 2 * SUBLANES
    n_blocks = max(nb for nb in range(1, n_steps + 1)
                   if n_steps % nb == 0 and rows % nb == 0 and (rows // nb) % bf16_rows == 0)
    per, blk = n_steps // n_blocks, rows // n_blocks
    in_spec = pl.BlockSpec((None, blk, cols), lambda *g: (layer, step_of(*g) // per, 0))
    out_spec = pl.BlockSpec((blk, cols), lambda *g: (step_of(*g) // per, 0))
    return in_spec, out_spec, jax.ShapeDtypeStruct((rows, cols), BF16), per


def _run_cast_sidecars(step, steps_per_block, in_refs, out_refs):
    for per, src, dst in zip(steps_per_block, in_refs, out_refs):
        @pl.when(step % per == 0)
        def _():
            dst[...] = src[...].astype(dst.dtype)


def _rms_rows(x, g):
    ms = jnp.mean(x * x, axis=-1, keepdims=True)
    return x * lax.rsqrt(ms + RMS_EPS) * g


def _sigmoid(x):
    return 1.0 / (1.0 + jnp.exp(-x))


def _dot(a, b):
    return jnp.dot(a, b, preferred_element_type=F32)


def _norm_to_scratch(dst_ref, src_ref, g, rows, dst_off=0, chunk=256):
    chunk = min(chunk, rows)

    def body(c, carry):
        r = pl.multiple_of(c * chunk, chunk)
        x = src_ref[pl.ds(r, chunk), :]
        dst_ref[pl.ds(dst_off + r, chunk), :] = _rms_rows(x, g).astype(dst_ref.dtype)
        return carry

    lax.fori_loop(0, rows // chunk, body, 0)


def _norm_glu_kernel(x_ref, g_ref, wa_ref, wg_ref, ba_ref, bg_ref, o_ref, xn_ref):
    @pl.when(pl.program_id(1) == 0)
    def _():
        _norm_to_scratch(xn_ref, x_ref, g_ref[...], x_ref.shape[0])

    xn = xn_ref[...]
    a = _dot(xn, wa_ref[...].astype(BF16)) + ba_ref[...]
    gate = _dot(xn, wg_ref[...].astype(BF16)) + bg_ref[...]
    o_ref[...] = (a * _sigmoid(gate)).astype(o_ref.dtype)


def _norm_glu(x, g, w, b, tm, tn):
    s, d = x.shape
    n = w.shape[1] // 2
    nj = n // tn
    return pl.pallas_call(
        _norm_glu_kernel,
        grid=(s // tm, nj),
        in_specs=[
            pl.BlockSpec((tm, d), lambda i, j: (i, 0)),
            pl.BlockSpec((1, d), lambda i, j: (0, 0)),
            pl.BlockSpec((d, tn), lambda i, j: (0, j)),
            pl.BlockSpec((d, tn), lambda i, j: (0, j + nj)),
            pl.BlockSpec((1, tn), lambda i, j: (0, j)),
            pl.BlockSpec((1, tn), lambda i, j: (0, j + nj)),
        ],
        out_specs=pl.BlockSpec((tm, tn), lambda i, j: (i, j)),
        out_shape=jax.ShapeDtypeStruct((s, n), F32),
        scratch_shapes=[pltpu.VMEM((tm, d), BF16)],
        compiler_params=_params(2),
        name="norm_glu",
    )(x, g, w, w, b, b)


def _norm_matmul_kernel(x_ref, g_ref, w_ref, cs_ref, o_ref, xn_ref):
    @pl.when(pl.program_id(1) == 0)
    def _():
        _norm_to_scratch(xn_ref, x_ref, g_ref[...], x_ref.shape[0])

    o_ref[...] = (_dot(xn_ref[...], w_ref[...].astype(BF16)) * cs_ref[...]).astype(o_ref.dtype)


def _norm_matmul(x, g, w, col_scale, tm, tn, out_dtype):
    s, d = x.shape
    n = w.shape[1]
    return pl.pallas_call(
        _norm_matmul_kernel,
        grid=(s // tm, n // tn),
        in_specs=[
            pl.BlockSpec((tm, d), lambda i, j: (i, 0)),
            pl.BlockSpec((1, d), lambda i, j: (0, 0)),
            pl.BlockSpec((d, tn), lambda i, j: (0, j)),
            pl.BlockSpec((1, tn), lambda i, j: (0, j)),
        ],
        out_specs=pl.BlockSpec((tm, tn), lambda i, j: (i, j)),
        out_shape=jax.ShapeDtypeStruct((s, n), out_dtype),
        scratch_shapes=[pltpu.VMEM((tm, d), BF16)],
        compiler_params=_params(2),
        name="norm_matmul",
    )(x, g, w, col_scale)


def _dwconv_window(win, w8_ref, lanes, rows, k):
    first = CONV_HALO - (k - 1)
    n_win = rows + CONV_HALO
    acc = None
    for b in range(SUBLANES):
        offs = [o for o in range(b, CONV_HALO + 1, SUBLANES) if 0 <= o - first < k]
        if not offs:
            continue
        shifted = win if b == 0 else pltpu.roll(win, n_win - b, axis=0)
        for o in offs:
            tap = pltpu.repeat(w8_ref[o - first, :, lanes], rows // SUBLANES, axis=0)
            term = shifted[o - b:o - b + rows] * tap
            acc = term if acc is None else acc + term
    return acc


def _conv_mix_kernel(u_ref, halo_ref, h_ref, wdw_ref, bdw_ref, lng_ref, lnb_ref, w2_ref, b2_ref,
                     pg_ref, cast_a_ref, cast_b_ref, o_ref, cast_a_out_ref, cast_b_out_ref,
                     ext_ref, act_ref, w8_ref, *, k, rows, cast_periods):
    tm, d = u_ref.shape
    _run_cast_sidecars(pl.program_id(0), cast_periods, (cast_a_ref, cast_b_ref),
                       (cast_a_out_ref, cast_b_out_ref))
    ext_ref[0:CONV_HALO, :] = jnp.where(pl.program_id(0) > 0, halo_ref[...], 0.0)
    ext_ref[CONV_HALO:, :] = u_ref[...]
    for j in range(k):
        w8_ref[j] = jnp.broadcast_to(wdw_ref[j:j + 1, :], (SUBLANES, d))

    n_lane = d // LANES

    def conv_body(c, carry):
        r0 = pl.multiple_of((c // n_lane) * rows, rows)
        lanes = pl.ds(pl.multiple_of((c % n_lane) * LANES, LANES), LANES)
        win = ext_ref[pl.ds(r0, rows + CONV_HALO), lanes]
        y = _dwconv_window(win, w8_ref, lanes, rows, k) + bdw_ref[:, lanes]
        ext_ref[pl.ds(r0, rows), lanes] = y
        return carry

    lax.fori_loop(0, (tm // rows) * n_lane, conv_body, 0)

    chunk = min(256, tm)

    def ln_body(c, carry):
        r = pl.multiple_of(c * chunk, chunk)
        x = ext_ref[pl.ds(r, chunk), :]
        mu = jnp.mean(x, axis=-1, keepdims=True)
        xc = x - mu
        y = xc * lax.rsqrt(jnp.mean(xc * xc, axis=-1, keepdims=True) + LN_EPS)
        y = y * lng_ref[...] + lnb_ref[...]
        act_ref[pl.ds(r, chunk), :] = (y * _sigmoid(y)).astype(act_ref.dtype)
        return carry

    lax.fori_loop(0, tm // chunk, ln_body, 0)

    y = _dot(act_ref[...], w2_ref[...]) + b2_ref[...]
    o_ref[...] = h_ref[...] + _rms_rows(y, pg_ref[...])


def _conv_mix(u, h, w_dw, b_dw, ln_g, ln_b, w2, b2, post_g, cast_a, cast_b, cast_layer, tm):
    s, d = u.shape
    k = w_dw.shape[0]
    halo_blocks = tm // CONV_HALO
    row = lambda i: (i, 0)
    n_steps = s // tm
    a_in, a_out, a_shape, a_per = _cast_sidecar(cast_a, cast_layer, n_steps, lambda i: i)
    b_in, b_out, b_shape, b_per = _cast_sidecar(cast_b, cast_layer, n_steps, lambda i: i)
    return pl.pallas_call(
        functools.partial(_conv_mix_kernel, k=k, rows=min(128, tm), cast_periods=(a_per, b_per)),
        grid=(n_steps,),
        in_specs=[
            pl.BlockSpec((tm, d), row),
            pl.BlockSpec((CONV_HALO, d), lambda i: (jnp.maximum(i * halo_blocks - 1, 0), 0)),
            pl.BlockSpec((tm, d), row),
            _resident((k, d)),
            _resident((1, d)),
            _resident((1, d)),
            _resident((1, d)),
            _resident((d, d)),
            _resident((1, d)),
            _resident((1, d)),
            a_in,
            b_in,
        ],
        out_specs=(pl.BlockSpec((tm, d), row), a_out, b_out),
        out_shape=(jax.ShapeDtypeStruct((s, d), F32), a_shape, b_shape),
        scratch_shapes=[pltpu.VMEM((tm + CONV_HALO, d), F32), pltpu.VMEM((tm, d), BF16),
                        pltpu.VMEM((k, SUBLANES, d), F32)],
        compiler_params=_params(1),
        name="conv_mix",
    )(u, u, h, w_dw, b_dw, ln_g, ln_b, w2, b2, post_g, cast_a, cast_b)


def _causal3(u, w, tm):
    return (u[FFN_HALO:] * w[2:3]
            + u[FFN_HALO - 1:FFN_HALO - 1 + tm] * w[1:2]
            + u[FFN_HALO - 2:FFN_HALO - 2 + tm] * w[0:1])


def _ffn_kernel(h_ref, halo_ref, g_ref, wg_ref, wu_ref, dwg_ref, dwu_ref, wo_ref, pg_ref,
                o_ref, xn_ref):
    i = pl.program_id(0)
    j = pl.program_id(1)
    tm = h_ref.shape[0]

    @pl.when(j == 0)
    def _():
        hist = jnp.where(i > 0, _rms_rows(halo_ref[...], g_ref[...]), 0.0)
        xn_ref[0:FFN_HALO, :] = hist.astype(xn_ref.dtype)
        _norm_to_scratch(xn_ref, h_ref, g_ref[...], tm, dst_off=FFN_HALO)
        o_ref[...] = jnp.zeros(o_ref.shape, o_ref.dtype)

    xn = xn_ref[...]
    gate = _causal3(_dot(xn, wg_ref[...]), dwg_ref[...], tm)
    up = _causal3(_dot(xn, wu_ref[...]), dwu_ref[...], tm)
    act = (gate * _sigmoid(gate) * up).astype(BF16)
    o_ref[...] += _dot(act, wo_ref[...])

    @pl.when(j == pl.num_programs(1) - 1)
    def _():
        chunk = min(256, tm)

        def body(c, carry):
            r = pl.multiple_of(c * chunk, chunk)
            rows = pl.ds(r, chunk)
            o_ref[rows, :] = h_ref[rows, :] + _rms_rows(o_ref[rows, :], pg_ref[...])
            return carry

        lax.fori_loop(0, tm // chunk, body, 0)


def _conv_ffn(h, pre_g, w_in, w_dw, w_out, layer, post_g, tm, tf):
    s, d = h.shape
    f = w_out.shape[0]
    nj = f // tf
    halo_blocks = tm // FFN_HALO
    return pl.pallas_call(
        _ffn_kernel,
        grid=(s // tm, nj),
        in_specs=[
            pl.BlockSpec((tm, d), lambda i, j: (i, 0), pipeline_mode=pl.Buffered(1)),
            pl.BlockSpec((FFN_HALO, d), lambda i, j: (jnp.maximum(i * halo_blocks - 1, 0), 0)),
            pl.BlockSpec((1, d), lambda i, j: (0, 0)),
            pl.BlockSpec((d, tf), lambda i, j: (0, j)),
            pl.BlockSpec((d, tf), lambda i, j: (0, j + nj)),
            pl.BlockSpec((None, 3, tf), lambda i, j: (layer, 0, j)),
            pl.BlockSpec((None, 3, tf), lambda i, j: (layer, 0, j + nj)),
            pl.BlockSpec((tf, d), lambda i, j: (j, 0)),
            pl.BlockSpec((1, d), lambda i, j: (0, 0)),
        ],
        out_specs=pl.BlockSpec((tm, d), lambda i, j: (i, 0)),
        out_shape=jax.ShapeDtypeStruct((s, d), F32),
        scratch_shapes=[pltpu.VMEM((tm + FFN_HALO, d), BF16)],
        compiler_params=_params(2),
        name="conv_ffn",
    )(h, h, pre_g, w_in, w_in, w_dw, w_dw, w_out, post_g)


def _ple_kernel(h_ref, p_ref, g_ref, wg_ref, bg_ref, wp_ref, pg_ref, o_ref):
    h = h_ref[...]
    xn = _rms_rows(h, g_ref[...]).astype(BF16)
    gate = _sigmoid(_dot(xn, wg_ref[...]) + bg_ref[...])
    emb = _dot(p_ref[...].astype(BF16), wp_ref[...])
    o_ref[...] = h + _rms_rows(gate * emb, pg_ref[...])


def _per_layer_embed(h, p, p_index, pre_g, w_gate, b_gate, w_proj, layer, post_g, tm):
    s, d = h.shape
    e = p.shape[2]
    row = lambda i: (i, 0)
    layer_resident = lambda shape: pl.BlockSpec((None,) + shape, lambda i: (layer, 0, 0),
                                                pipeline_mode=pl.Buffered(1))
    return pl.pallas_call(
        _ple_kernel,
        grid=(s // tm,),
        in_specs=[
            pl.BlockSpec((tm, d), row),
            pl.BlockSpec((None, tm, e), lambda i: (p_index, i, 0)),
            _resident((1, d)),
            layer_resident((d, d)),
            _resident((1, d)),
            layer_resident((e, d)),
            _resident((1, d)),
        ],
        out_specs=pl.BlockSpec((tm, d), row),
        out_shape=jax.ShapeDtypeStruct((s, d), F32),
        compiler_params=_params(1),
        name="per_layer_embed",
    )(h, p, pre_g, w_gate, b_gate, w_proj, post_g)


L_FLOOR = 2.0 ** -64
TILES_PER_TRIP = 4


def _diff_attn_kernel(q_ref, k_ref, v_ref, lq1_ref, lk1_ref, lq2_ref, lk2_ref, sg_ref,
                      cast_a_ref, cast_b_ref, o_ref, cast_a_out_ref, cast_b_out_ref,
                      b_ref, l_ref, acc_ref, kn_ref, *, lambda_init, cast_periods):
    qi = pl.program_id(1)
    tq = q_ref.shape[0]
    tk = tq
    n_lane_tiles = tk // LANES
    n_keys = k_ref.shape[0]
    _run_cast_sidecars(pl.program_id(0) * pl.num_programs(1) + qi, cast_periods,
                       (cast_a_ref, cast_b_ref), (cast_a_out_ref, cast_b_out_ref))

    def map_lanes(c):
        return slice(c * HEAD_DIM, (c + 1) * HEAD_DIM)

    def scores(k0, c, diagonal):
        s = lax.dot_general(q_ref[:, map_lanes(c)], k_ref[pl.ds(k0, tk), map_lanes(c)],
                            (((1,), (1,)), ((), ())), preferred_element_type=F32)
        if diagonal:
            rows = lax.broadcasted_iota(jnp.int32, s.shape, 0)
            cols = lax.broadcasted_iota(jnp.int32, s.shape, 1)
            s = jnp.where(rows >= cols, s, MASK_VALUE)
        return s

    @pl.when(qi == 0)
    def _():
        chunk = min(512, n_keys)
        for c in range(2):
            def body(i, best):
                rows = pl.ds(pl.multiple_of(i * chunk, chunk), chunk)
                kk = k_ref[rows, map_lanes(c)].astype(F32)
                return jnp.maximum(best, jnp.sum(kk * kk, axis=1, keepdims=True))

            best = lax.fori_loop(0, n_keys // chunk, body, jnp.zeros((chunk, 1), F32))
            kn_ref[c] = jnp.broadcast_to(jnp.sqrt(jnp.max(best, axis=0, keepdims=True)),
                                         (SUBLANES, LANES))

    def tile(k0, diagonal):
        v = v_ref[pl.ds(k0, tk), :]
        for c in range(2):
            p = jnp.exp2(scores(k0, c, diagonal) - pltpu.repeat(b_ref[c], n_lane_tiles, axis=1))
            psum = p[:, 0:LANES]
            for t in range(1, n_lane_tiles):
                psum = psum + p[:, t * LANES:(t + 1) * LANES]
            l_ref[c] += psum
            acc_ref[c] += _dot(p.astype(BF16), v)

    def start(t):
        return pl.multiple_of(t * tk, tk)

    def sweep():
        l_ref[...] = jnp.zeros(l_ref.shape, F32)
        acc_ref[...] = jnp.zeros(acc_ref.shape, F32)

        def trip(g, carry):
            for u in range(TILES_PER_TRIP):
                tile(start(TILES_PER_TRIP * g + u), False)
            return carry

        lax.fori_loop(0, qi // TILES_PER_TRIP, trip, 0)
        base = (qi // TILES_PER_TRIP) * TILES_PER_TRIP
        rem = qi % TILES_PER_TRIP

        @pl.when(rem >= 2)
        def _():
            tile(start(base), False)
            tile(start(base + 1), False)

        last = base + jnp.where(rem >= 2, 2, 0)

        @pl.when(rem % 2 == 1)
        def _():
            tile(start(last), False)
            tile(start(qi), True)

        @pl.when(rem % 2 == 0)
        def _():
            tile(start(qi), True)

    def exact_row_max():
        for c in range(2):
            def fold(best, s):
                for u in range(n_lane_tiles):
                    best = jnp.maximum(best, s[:, u * LANES:(u + 1) * LANES])
                return best

            best = lax.fori_loop(0, qi, lambda t, best: fold(best, scores(start(t), c, False)),
                                 jnp.full((tq, LANES), MASK_VALUE, F32))
            best = fold(best, scores(start(qi), c, True))
            b_ref[c] = jnp.broadcast_to(jnp.max(best, axis=1, keepdims=True), (tq, LANES))

    for c in range(2):
        qq = q_ref[:, map_lanes(c)].astype(F32)
        b_ref[c] = jnp.sqrt(jnp.sum(qq * qq, axis=1, keepdims=True)) * kn_ref[c, 0:1, :]
    sweep()

    l_min = jnp.min(jnp.minimum(jnp.sum(l_ref[0], axis=1, keepdims=True),
                                jnp.sum(l_ref[1], axis=1, keepdims=True)))

    @pl.when(jnp.logical_not(l_min >= L_FLOOR))
    def _():
        exact_row_max()
        sweep()

    lam = (jnp.exp(jnp.sum(lq1_ref[...] * lk1_ref[...], keepdims=True))
           - jnp.exp(jnp.sum(lq2_ref[...] * lk2_ref[...], keepdims=True)) + lambda_init)
    l0 = jnp.sum(l_ref[0], axis=1, keepdims=True)
    l1 = jnp.sum(l_ref[1], axis=1, keepdims=True)
    o = acc_ref[0] / l0 - lam * (acc_ref[1] / l1)
    o_ref[...] = (_rms_rows(o, sg_ref[...]) * (1.0 - lambda_init)).astype(o_ref.dtype)


def _diff_attention(qkv, lq1, lk1, lq2, lk2, subln_g, lambda_init, cast_a, cast_b, cast_layer, tq):
    s = qkv.shape[0]
    vd = 2 * HEAD_DIM
    heads = qkv.shape[1] // (3 * vd)
    nq = s // tq
    vec = pl.BlockSpec((1, HEAD_DIM), lambda h, i: (0, 0))
    step_of = lambda h, i: h * nq + i
    a_in, a_out, a_shape, a_per = _cast_sidecar(cast_a, cast_layer, heads * nq, step_of)
    b_in, b_out, b_shape, b_per = _cast_sidecar(cast_b, cast_layer, heads * nq, step_of)
    return pl.pallas_call(
        functools.partial(_diff_attn_kernel, lambda_init=lambda_init,
                          cast_periods=(a_per, b_per)),
        grid=(heads, nq),
        in_specs=[
            pl.BlockSpec((tq, vd), lambda h, i: (i, h)),
            pl.BlockSpec((s, vd), lambda h, i: (0, heads + h)),
            pl.BlockSpec((s, vd), lambda h, i: (0, 2 * heads + h)),
            vec, vec, vec, vec,
            pl.BlockSpec((1, vd), lambda h, i: (0, 0)),
            a_in,
            b_in,
        ],
        out_specs=(pl.BlockSpec((tq, vd), lambda h, i: (i, h)), a_out, b_out),
        out_shape=(jax.ShapeDtypeStruct((s, heads * vd), BF16), a_shape, b_shape),
        scratch_shapes=[pltpu.VMEM((2, tq, LANES), F32), pltpu.VMEM((2, tq, LANES), F32),
                        pltpu.VMEM((2, tq, vd), F32), pltpu.VMEM((2, SUBLANES, LANES), F32)],
        compiler_params=_params(2),
        name="diff_attention",
    )(qkv, qkv, qkv, lq1, lk1, lq2, lk2, subln_g, cast_a, cast_b)


def _proj_residual_kernel(a_ref, h_ref, w_ref, pg_ref, o_ref):
    o_ref[...] = h_ref[...] + _rms_rows(_dot(a_ref[...], w_ref[...]), pg_ref[...])


def _proj_residual(a, h, w, post_g, tm):
    s, d = h.shape
    row = lambda i: (i, 0)
    return pl.pallas_call(
        _proj_residual_kernel,
        grid=(s // tm,),
        in_specs=[pl.BlockSpec((tm, a.shape[1]), row), pl.BlockSpec((tm, d), row),
                  _resident(w.shape), _resident((1, d))],
        out_specs=pl.BlockSpec((tm, d), row),
        out_shape=jax.ShapeDtypeStruct((s, d), F32),
        compiler_params=_params(1),
        name="proj_residual",
    )(a, h, w, post_g)


def _tiles(s, d, f):
    ffn_cols = 512 if f % 512 == 0 else 256
    return dict(
        proj_rows=min(1024, s), proj_cols=min(512, d), qkv_cols=min(1024, d),
        conv_rows=min(256, s),
        ffn_rows=min(1024, s), ffn_cols=min(ffn_cols, f),
        ple_rows=min(512, s),
        attn_rows=min(512, s),
        out_rows=min(512, s),
    )


def kernel(x, p, pre_mix_g, post_mix_g, conv_w_pw1, conv_b_pw1, conv_w_dw, conv_b_dw, conv_ln_g, conv_ln_b, conv_w_pw2, conv_b_pw2, attn_w_qkv, attn_lambda_q1, attn_lambda_k1, attn_lambda_q2, attn_lambda_k2, attn_subln_g, attn_w_o, pre_ffn_g, post_ffn_g, ffn_w_in, ffn_w_dw, ffn_w_out, ple_pre_g, ple_w_gate, ple_b_gate, ple_w_proj, ple_post_g):
    batch, s, d = x.shape
    depth = p.shape[0]
    f = ffn_w_out.shape[1]
    t = _tiles(s, d, f)
    vec = lambda a: a.reshape(1, -1)
    bf = lambda a: a.astype(BF16)
    p_all = p.reshape(depth * batch, s, p.shape[-1])
    ple_w_gate_bf, ple_w_proj_bf = bf(ple_w_gate), bf(ple_w_proj)

    outs = []
    for b in range(batch):
        h = x.reshape(s, d) if batch == 1 else x[b]
        for i in range(depth):
            j = i // 2
            if i % 2 == 0:
                u = _norm_glu(h, vec(pre_mix_g[i]), conv_w_pw1[j], vec(conv_b_pw1[j]),
                              t["proj_rows"], t["proj_cols"])
                h, w_in_bf, w_out_bf = _conv_mix(
                    u, h, conv_w_dw[j], vec(conv_b_dw[j]), vec(conv_ln_g[j]), vec(conv_ln_b[j]),
                    bf(conv_w_pw2[j]), vec(conv_b_pw2[j]), vec(post_mix_g[i]),
                    ffn_w_in, ffn_w_out, i, t["conv_rows"])
            else:
                q_scale = HEAD_DIM ** -0.5 * math.log2(math.e)
                col_scale = jnp.concatenate([jnp.full((1, d), q_scale, F32),
                                             jnp.ones((1, 2 * d), F32)], axis=1)
                qkv = _norm_matmul(h, vec(pre_mix_g[i]), attn_w_qkv[j], col_scale,
                                   t["proj_rows"], t["qkv_cols"], BF16)
                lambda_init = 0.8 - 0.6 * math.exp(-0.3 * i)
                o, w_in_bf, w_out_bf = _diff_attention(
                    qkv, vec(attn_lambda_q1[j]), vec(attn_lambda_k1[j]), vec(attn_lambda_q2[j]),
                    vec(attn_lambda_k2[j]), vec(attn_subln_g[j]), lambda_init,
                    ffn_w_in, ffn_w_out, i, t["attn_rows"])
                h = _proj_residual(o, h, bf(attn_w_o[j]), vec(post_mix_g[i]), t["out_rows"])
            h = _conv_ffn(h, vec(pre_ffn_g[i]), w_in_bf, ffn_w_dw, w_out_bf, i,
                          vec(post_ffn_g[i]), t["ffn_rows"], t["ffn_cols"])
            h = _per_layer_embed(h, p_all, i * batch + b, vec(ple_pre_g[i]), ple_w_gate_bf,
                                 vec(ple_b_gate[i]), ple_w_proj_bf, i, vec(ple_post_g[i]),
                                 t["ple_rows"])
        outs.append(h)
    return outs[0].reshape(1, s, d) if batch == 1 else jnp.stack(outs)
```

```python
import functools
import math

import jax
import jax.numpy as jnp
from jax import lax
from jax.experimental import pallas as pl
from jax.experimental.pallas import tpu as pltpu

F32 = jnp.float32
BF16 = jnp.bfloat16

RMS_EPS = 1e-6
LN_EPS = 1e-5
MASK_VALUE = -1e30
HEAD_DIM = 128
CONV_HALO = 32
SUBLANES = 8
LANES = 128
V7X_VMEM_LIMIT_BYTES = 56 * 1024 * 1024


def _params(n_axes, vmem_bytes=V7X_VMEM_LIMIT_BYTES):
    return pltpu.CompilerParams(dimension_semantics=("arbitrary",) * n_axes,
                                vmem_limit_bytes=vmem_bytes)


def _resident(shape):
    return pl.BlockSpec(shape, lambda *_: (0,) * len(shape), pipeline_mode=pl.Buffered(1))


def _cast_sidecar(w, layer, n_steps, step_of):
    rows, cols = w.shape[1:]
    bf16_rows = 2 * SUBLANES
    n_blocks = max(nb for nb in range(1, n_steps + 1)
                   if n_steps % nb == 0 and rows % nb == 0 and (rows // nb) % bf16_rows == 0)
    per, blk = n_steps // n_blocks, rows // n_blocks
    in_spec = pl.BlockSpec((None, blk, cols), lambda *g: (layer, step_of(*g) // per, 0))
    out_spec = pl.BlockSpec((blk, cols), lambda *g: (step_of(*g) // per, 0))
    return in_spec, out_spec, jax.ShapeDtypeStruct((rows, cols), BF16), per


def _host_call(body, casts, grid, in_specs, out_spec, out_shape, scratch_shapes, name, args):
    n_steps = math.prod(grid)

    def step_of(*g):
        step = 0
        for idx, extent in zip(g, grid):
            step = step * extent + idx
        return step

    sidecars = [_cast_sidecar(w, layer, n_steps, step_of) for w, layer in casts]
    n_in, n_cast = len(in_specs), len(casts)

    def kernel(*refs):
        ins, cast_ins = refs[:n_in], refs[n_in:n_in + n_cast]
        out, cast_outs = refs[n_in + n_cast], refs[n_in + n_cast + 1:n_in + 2 * n_cast + 1]
        scratch = refs[n_in + 2 * n_cast + 1:]
        step = step_of(*(pl.program_id(a) for a in range(len(grid))))
        for (_, _, _, per), src, dst in zip(sidecars, cast_ins, cast_outs):
            @pl.when(step % per == 0)
            def _():
                dst[...] = src[...].astype(dst.dtype)
        body(*ins, out, *scratch)

    return pl.pallas_call(
        kernel,
        grid=grid,
        in_specs=list(in_specs) + [sc[0] for sc in sidecars],
        out_specs=(out_spec,) + tuple(sc[1] for sc in sidecars),
        out_shape=(out_shape,) + tuple(sc[2] for sc in sidecars),
        scratch_shapes=scratch_shapes,
        compiler_params=_params(len(grid)),
        name=name,
    )(*args, *(w for w, _ in casts))


def _rms_rows(x, g):
    ms = jnp.mean(x * x, axis=-1, keepdims=True)
    return x * lax.rsqrt(ms + RMS_EPS) * g


def _sigmoid(x):
    return 1.0 / (1.0 + jnp.exp(-x))


def _dot(a, b):
    return jnp.dot(a, b, preferred_element_type=F32)


def _norm_to_scratch(dst_ref, src_ref, g, rows, dst_off=0, chunk=256):
    chunk = min(chunk, rows)

    def body(c, carry):
        r = pl.multiple_of(c * chunk, chunk)
        x = src_ref[pl.ds(r, chunk), :]
        dst_ref[pl.ds(dst_off + r, chunk), :] = _rms_rows(x, g).astype(dst_ref.dtype)
        return carry

    lax.fori_loop(0, rows // chunk, body, 0)


def _norm_glu_kernel(x_ref, g_ref, wa_ref, wg_ref, ba_ref, bg_ref, o_ref, xn_ref):
    @pl.when(pl.program_id(1) == 0)
    def _():
        _norm_to_scratch(xn_ref, x_ref, g_ref[...], x_ref.shape[0])

    xn = xn_ref[...]
    a = _dot(xn, wa_ref[...].astype(BF16)) + ba_ref[...]
    gate = _dot(xn, wg_ref[...].astype(BF16)) + bg_ref[...]
    o_ref[...] = (a * _sigmoid(gate)).astype(o_ref.dtype)


def _norm_glu(x, g, w, b, casts, tm, tn):
    s, d = x.shape
    n = w.shape[1] // 2
    nj = n // tn
    return _host_call(
        _norm_glu_kernel, casts,
        grid=(s // tm, nj),
        in_specs=[
            pl.BlockSpec((tm, d), lambda i, j: (i, 0)),
            pl.BlockSpec((1, d), lambda i, j: (0, 0)),
            pl.BlockSpec((d, tn), lambda i, j: (0, j)),
            pl.BlockSpec((d, tn), lambda i, j: (0, j + nj)),
            pl.BlockSpec((1, tn), lambda i, j: (0, j)),
            pl.BlockSpec((1, tn), lambda i, j: (0, j + nj)),
        ],
        out_spec=pl.BlockSpec((tm, tn), lambda i, j: (i, j)),
        out_shape=jax.ShapeDtypeStruct((s, n), F32),
        scratch_shapes=[pltpu.VMEM((tm, d), BF16)],
        name="norm_glu",
        args=(x, g, w, w, b, b))


def _norm_matmul_kernel(x_ref, g_ref, w_ref, cs_ref, o_ref, xn_ref):
    @pl.when(pl.program_id(1) == 0)
    def _():
        _norm_to_scratch(xn_ref, x_ref, g_ref[...], x_ref.shape[0])

    o_ref[...] = (_dot(xn_ref[...], w_ref[...].astype(BF16)) * cs_ref[...]).astype(o_ref.dtype)


def _norm_matmul(x, g, w, col_scale, casts, tm, tn, out_dtype):
    s, d = x.shape
    n = w.shape[1]
    return _host_call(
        _norm_matmul_kernel, casts,
        grid=(s // tm, n // tn),
        in_specs=[
            pl.BlockSpec((tm, d), lambda i, j: (i, 0)),
            pl.BlockSpec((1, d), lambda i, j: (0, 0)),
            pl.BlockSpec((d, tn), lambda i, j: (0, j)),
            pl.BlockSpec((1, tn), lambda i, j: (0, j)),
        ],
        out_spec=pl.BlockSpec((tm, tn), lambda i, j: (i, j)),
        out_shape=jax.ShapeDtypeStruct((s, n), out_dtype),
        scratch_shapes=[pltpu.VMEM((tm, d), BF16)],
        name="norm_matmul",
        args=(x, g, w, col_scale))


def _dwconv_window(win, w8_ref, lanes, rows, k):
    first = CONV_HALO - (k - 1)
    n_win = rows + CONV_HALO
    acc = None
    for b in range(SUBLANES):
        offs = [o for o in range(b, CONV_HALO + 1, SUBLANES) if 0 <= o - first < k]
        if not offs:
            continue
        shifted = win if b == 0 else pltpu.roll(win, n_win - b, axis=0)
        for o in offs:
            tap = pltpu.repeat(w8_ref[o - first, :, lanes], rows // SUBLANES, axis=0)
            term = shifted[o - b:o - b + rows] * tap
            acc = term if acc is None else acc + term
    return acc


def _conv_mix_kernel(u_ref, halo_ref, h_ref, wdw_ref, bdw_ref, lng_ref, lnb_ref, w2_ref, b2_ref,
                     pg_ref, o_ref, ext_ref, act_ref, w8_ref, *, k, rows):
    tm, d = u_ref.shape
    ext_ref[0:CONV_HALO, :] = jnp.where(pl.program_id(0) > 0, halo_ref[...], 0.0)
    ext_ref[CONV_HALO:, :] = u_ref[...]
    for j in range(k):
        w8_ref[j] = jnp.broadcast_to(wdw_ref[j:j + 1, :], (SUBLANES, d))

    n_lane = d // LANES

    def conv_body(c, carry):
        r0 = pl.multiple_of((c // n_lane) * rows, rows)
        lanes = pl.ds(pl.multiple_of((c % n_lane) * LANES, LANES), LANES)
        win = ext_ref[pl.ds(r0, rows + CONV_HALO), lanes]
        y = _dwconv_window(win, w8_ref, lanes, rows, k) + bdw_ref[:, lanes]
        ext_ref[pl.ds(r0, rows), lanes] = y
        return carry

    lax.fori_loop(0, (tm // rows) * n_lane, conv_body, 0)

    chunk = min(256, tm)

    def ln_body(c, carry):
        r = pl.multiple_of(c * chunk, chunk)
        x = ext_ref[pl.ds(r, chunk), :]
        mu = jnp.mean(x, axis=-1, keepdims=True)
        xc = x - mu
        y = xc * lax.rsqrt(jnp.mean(xc * xc, axis=-1, keepdims=True) + LN_EPS)
        y = y * lng_ref[...] + lnb_ref[...]
        act_ref[pl.ds(r, chunk), :] = (y * _sigmoid(y)).astype(act_ref.dtype)
        return carry

    lax.fori_loop(0, tm // chunk, ln_body, 0)

    y = _dot(act_ref[...], w2_ref[...]) + b2_ref[...]
    o_ref[...] = h_ref[...] + _rms_rows(y, pg_ref[...])


def _conv_mix(u, h, w_dw, b_dw, ln_g, ln_b, w2, b2, post_g, casts, tm):
    s, d = u.shape
    k = w_dw.shape[0]
    halo_blocks = tm // CONV_HALO
    row = lambda i: (i, 0)
    return _host_call(
        functools.partial(_conv_mix_kernel, k=k, rows=min(128, tm)), casts,
        grid=(s // tm,),
        in_specs=[
            pl.BlockSpec((tm, d), row),
            pl.BlockSpec((CONV_HALO, d), lambda i: (jnp.maximum(i * halo_blocks - 1, 0), 0)),
            pl.BlockSpec((tm, d), row),
            _resident((k, d)),
            _resident((1, d)),
            _resident((1, d)),
            _resident((1, d)),
            _resident((d, d)),
            _resident((1, d)),
            _resident((1, d)),
        ],
        out_spec=pl.BlockSpec((tm, d), row),
        out_shape=jax.ShapeDtypeStruct((s, d), F32),
        scratch_shapes=[pltpu.VMEM((tm + CONV_HALO, d), F32), pltpu.VMEM((tm, d), BF16),
                        pltpu.VMEM((k, SUBLANES, d), F32)],
        name="conv_mix",
        args=(u, u, h, w_dw, b_dw, ln_g, ln_b, w2, b2, post_g))


def _causal3(u, tail, w):
    tm = u.shape[0]
    ext = jnp.concatenate([tail, u], axis=0)
    return (u * w[2:3]
            + ext[SUBLANES - 1:SUBLANES - 1 + tm] * w[1:2]
            + ext[SUBLANES - 2:SUBLANES - 2 + tm] * w[0:1])


def _ffn_kernel(h_ref, g_ref, wg_ref, wu_ref, dwg_ref, dwu_ref, wo_ref, pg_ref,
                o_ref, xn_ref, tail_ref):
    i = pl.program_id(0)
    j = pl.program_id(1)
    tm = h_ref.shape[0]

    @pl.when(j == 0)
    def _():
        _norm_to_scratch(xn_ref, h_ref, g_ref[...], tm)
        o_ref[...] = jnp.zeros(o_ref.shape, o_ref.dtype)

    @pl.when(i == 0)
    def _():
        tail_ref[j] = jnp.zeros(tail_ref.shape[1:], F32)

    xn = xn_ref[...]
    pre_gate = _dot(xn, wg_ref[...])
    pre_up = _dot(xn, wu_ref[...])
    gate = _causal3(pre_gate, tail_ref[j, 0], dwg_ref[...])
    up = _causal3(pre_up, tail_ref[j, 1], dwu_ref[...])
    tail_ref[j, 0] = pre_gate[tm - SUBLANES:]
    tail_ref[j, 1] = pre_up[tm - SUBLANES:]
    act = (gate * _sigmoid(gate) * up).astype(BF16)
    o_ref[...] += _dot(act, wo_ref[...])

    @pl.when(j == pl.num_programs(1) - 1)
    def _():
        chunk = min(256, tm)

        def body(c, carry):
            r = pl.multiple_of(c * chunk, chunk)
            rows = pl.ds(r, chunk)
            o_ref[rows, :] = h_ref[rows, :] + _rms_rows(o_ref[rows, :], pg_ref[...])
            return carry

        lax.fori_loop(0, tm // chunk, body, 0)


def _conv_ffn(h, pre_g, w_in, w_dw, w_out, layer, post_g, tm, tf):
    s, d = h.shape
    f = w_out.shape[0]
    nj = f // tf
    return pl.pallas_call(
        _ffn_kernel,
        grid=(s // tm, nj),
        in_specs=[
            pl.BlockSpec((tm, d), lambda i, j: (i, 0), pipeline_mode=pl.Buffered(1)),
            pl.BlockSpec((1, d), lambda i, j: (0, 0)),
            pl.BlockSpec((d, tf), lambda i, j: (0, j)),
            pl.BlockSpec((d, tf), lambda i, j: (0, j + nj)),
            pl.BlockSpec((None, 3, tf), lambda i, j: (layer, 0, j)),
            pl.BlockSpec((None, 3, tf), lambda i, j: (layer, 0, j + nj)),
            pl.BlockSpec((tf, d), lambda i, j: (j, 0)),
            pl.BlockSpec((1, d), lambda i, j: (0, 0)),
        ],
        out_specs=pl.BlockSpec((tm, d), lambda i, j: (i, 0)),
        out_shape=jax.ShapeDtypeStruct((s, d), F32),
        scratch_shapes=[pltpu.VMEM((tm, d), BF16), pltpu.VMEM((nj, 2, SUBLANES, tf), F32)],
        compiler_params=_params(2),
        name="conv_ffn",
    )(h, pre_g, w_in, w_in, w_dw, w_dw, w_out, post_g)


def _ple_kernel(h_ref, p_ref, g_ref, wg_ref, bg_ref, wp_ref, pg_ref, o_ref):
    h = h_ref[...]
    xn = _rms_rows(h, g_ref[...]).astype(BF16)
    gate = _sigmoid(_dot(xn, wg_ref[...]) + bg_ref[...])
    emb = _dot(p_ref[...].astype(BF16), wp_ref[...])
    o_ref[...] = h + _rms_rows(gate * emb, pg_ref[...])


def _per_layer_embed(h, p, p_index, pre_g, w_gate, b_gate, w_proj, post_g, tm):
    s, d = h.shape
    e = p.shape[2]
    row = lambda i: (i, 0)
    return pl.pallas_call(
        _ple_kernel,
        grid=(s // tm,),
        in_specs=[
            pl.BlockSpec((tm, d), row),
            pl.BlockSpec((None, tm, e), lambda i: (p_index, i, 0)),
            _resident((1, d)),
            _resident((d, d)),
            _resident((1, d)),
            _resident((e, d)),
            _resident((1, d)),
        ],
        out_specs=pl.BlockSpec((tm, d), row),
        out_shape=jax.ShapeDtypeStruct((s, d), F32),
        compiler_params=_params(1),
        name="per_layer_embed",
    )(h, p, pre_g, w_gate, b_gate, w_proj, post_g)


L_FLOOR = 2.0 ** -64
TILES_PER_TRIP = 4


def _diff_attn_kernel(q_ref, k_ref, v_ref, lq1_ref, lk1_ref, lq2_ref, lk2_ref, sg_ref, o_ref,
                      b_ref, l_ref, acc_ref, kn_ref, *, lambda_init):
    qi = pl.program_id(1)
    tq = q_ref.shape[0]
    tk = tq
    n_lane_tiles = tk // LANES
    n_keys = k_ref.shape[0]

    def map_lanes(c):
        return slice(c * HEAD_DIM, (c + 1) * HEAD_DIM)

    def scores(k0, c, diagonal):
        s = lax.dot_general(q_ref[:, map_lanes(c)], k_ref[pl.ds(k0, tk), map_lanes(c)],
                            (((1,), (1,)), ((), ())), preferred_element_type=F32)
        if diagonal:
            rows = lax.broadcasted_iota(jnp.int32, s.shape, 0)
            cols = lax.broadcasted_iota(jnp.int32, s.shape, 1)
            s = jnp.where(rows >= cols, s, MASK_VALUE)
        return s

    @pl.when(qi == 0)
    def _():
        chunk = min(512, n_keys)
        for c in range(2):
            def body(i, best):
                rows = pl.ds(pl.multiple_of(i * chunk, chunk), chunk)
                kk = k_ref[rows, map_lanes(c)].astype(F32)
                return jnp.maximum(best, jnp.sum(kk * kk, axis=1, keepdims=True))

            best = lax.fori_loop(0, n_keys // chunk, body, jnp.zeros((chunk, 1), F32))
            kn_ref[c] = jnp.broadcast_to(jnp.sqrt(jnp.max(best, axis=0, keepdims=True)),
                                         (SUBLANES, LANES))

    def block(row0, n_rows, k0, n_keys_blk, triangle):
        rows = slice(row0, row0 + n_rows)
        keys = pl.ds(k0, n_keys_blk)
        v = v_ref[keys, :]
        for c in range(2):
            s = lax.dot_general(q_ref[rows, map_lanes(c)], k_ref[keys, map_lanes(c)],
                                (((1,), (1,)), ((), ())), preferred_element_type=F32)
            if triangle:
                r_idx = lax.broadcasted_iota(jnp.int32, s.shape, 0)
                c_idx = lax.broadcasted_iota(jnp.int32, s.shape, 1)
                s = jnp.where(r_idx >= c_idx, s, MASK_VALUE)
            p = jnp.exp2(s - pltpu.repeat(b_ref[c, rows, :], n_keys_blk // LANES, axis=1))
            psum = p[:, 0:LANES]
            for t in range(1, n_keys_blk // LANES):
                psum = psum + p[:, t * LANES:(t + 1) * LANES]
            l_ref[c, rows, :] += psum
            acc_ref[c, rows, :] += _dot(p.astype(BF16), v)

    def tile(k0, diagonal):
        if not diagonal:
            block(0, tq, k0, tk, False)
        else:
            half = tk // 2
            block(0, tq, k0, half, True)
            block(half, tq - half, pl.multiple_of(k0 + half, half), half, True)

    def start(t):
        return pl.multiple_of(t * tk, tk)

    def sweep():
        l_ref[...] = jnp.zeros(l_ref.shape, F32)
        acc_ref[...] = jnp.zeros(acc_ref.shape, F32)

        def trip(g, carry):
            for u in range(TILES_PER_TRIP):
                tile(start(TILES_PER_TRIP * g + u), False)
            return carry

        lax.fori_loop(0, qi // TILES_PER_TRIP, trip, 0)
        base = (qi // TILES_PER_TRIP) * TILES_PER_TRIP
        rem = qi % TILES_PER_TRIP

        @pl.when(rem >= 2)
        def _():
            tile(start(base), False)
            tile(start(base + 1), False)

        last = base + jnp.where(rem >= 2, 2, 0)

        @pl.when(rem % 2 == 1)
        def _():
            tile(start(last), False)
            tile(start(qi), True)

        @pl.when(rem % 2 == 0)
        def _():
            tile(start(qi), True)

    def exact_row_max():
        for c in range(2):
            def fold(best, s):
                for u in range(n_lane_tiles):
                    best = jnp.maximum(best, s[:, u * LANES:(u + 1) * LANES])
                return best

            best = lax.fori_loop(0, qi, lambda t, best: fold(best, scores(start(t), c, False)),
                                 jnp.full((tq, LANES), MASK_VALUE, F32))
            best = fold(best, scores(start(qi), c, True))
            b_ref[c] = jnp.broadcast_to(jnp.max(best, axis=1, keepdims=True), (tq, LANES))

    for c in range(2):
        qq = q_ref[:, map_lanes(c)].astype(F32)
        b_ref[c] = jnp.sqrt(jnp.sum(qq * qq, axis=1, keepdims=True)) * kn_ref[c, 0:1, :]
    sweep()

    l_min = jnp.min(jnp.minimum(jnp.sum(l_ref[0], axis=1, keepdims=True),
                                jnp.sum(l_ref[1], axis=1, keepdims=True)))

    @pl.when(jnp.logical_not(l_min >= L_FLOOR))
    def _():
        exact_row_max()
        sweep()

    lam = (jnp.exp(jnp.sum(lq1_ref[...] * lk1_ref[...], keepdims=True))
           - jnp.exp(jnp.sum(lq2_ref[...] * lk2_ref[...], keepdims=True)) + lambda_init)
    l0 = jnp.sum(l_ref[0], axis=1, keepdims=True)
    l1 = jnp.sum(l_ref[1], axis=1, keepdims=True)
    o = acc_ref[0] / l0 - lam * (acc_ref[1] / l1)
    o_ref[...] = (_rms_rows(o, sg_ref[...]) * (1.0 - lambda_init)).astype(o_ref.dtype)


def _diff_attention(qkv, lq1, lk1, lq2, lk2, subln_g, lambda_init, casts, tq):
    s = qkv.shape[0]
    vd = 2 * HEAD_DIM
    heads = qkv.shape[1] // (3 * vd)
    vec = pl.BlockSpec((1, HEAD_DIM), lambda h, i: (0, 0))
    return _host_call(
        functools.partial(_diff_attn_kernel, lambda_init=lambda_init), casts,
        grid=(heads, s // tq),
        in_specs=[
            pl.BlockSpec((tq, vd), lambda h, i: (i, h)),
            pl.BlockSpec((s, vd), lambda h, i: (0, heads + h)),
            pl.BlockSpec((s, vd), lambda h, i: (0, 2 * heads + h)),
            vec, vec, vec, vec,
            pl.BlockSpec((1, vd), lambda h, i: (0, 0)),
        ],
        out_spec=pl.BlockSpec((tq, vd), lambda h, i: (i, h)),
        out_shape=jax.ShapeDtypeStruct((s, heads * vd), BF16),
        scratch_shapes=[pltpu.VMEM((2, tq, LANES), F32), pltpu.VMEM((2, tq, LANES), F32),
                        pltpu.VMEM((2, tq, vd), F32), pltpu.VMEM((2, SUBLANES, LANES), F32)],
        name="diff_attention",
        args=(qkv, qkv, qkv, lq1, lk1, lq2, lk2, subln_g))


def _proj_residual_kernel(a_ref, h_ref, w_ref, pg_ref, o_ref):
    o_ref[...] = h_ref[...] + _rms_rows(_dot(a_ref[...], w_ref[...]), pg_ref[...])


def _proj_residual(a, h, w, post_g, tm):
    s, d = h.shape
    row = lambda i: (i, 0)
    return pl.pallas_call(
        _proj_residual_kernel,
        grid=(s // tm,),
        in_specs=[pl.BlockSpec((tm, a.shape[1]), row), pl.BlockSpec((tm, d), row),
                  _resident(w.shape), _resident((1, d))],
        out_specs=pl.BlockSpec((tm, d), row),
        out_shape=jax.ShapeDtypeStruct((s, d), F32),
        compiler_params=_params(1),
        name="proj_residual",
    )(a, h, w, post_g)


def _tiles(s, d, f):
    ffn_cols = 512 if f % 512 == 0 else 256
    return dict(
        proj_rows=min(1024, s), proj_cols=min(512, d), qkv_cols=min(1024, d),
        conv_rows=min(256, s),
        ffn_rows=min(1024, s), ffn_cols=min(ffn_cols, f),
        ple_rows=min(512, s),
        attn_rows=min(512, s),
        out_rows=min(512, s),
    )


def kernel(x, p, pre_mix_g, post_mix_g, conv_w_pw1, conv_b_pw1, conv_w_dw, conv_b_dw, conv_ln_g, conv_ln_b, conv_w_pw2, conv_b_pw2, attn_w_qkv, attn_lambda_q1, attn_lambda_k1, attn_lambda_q2, attn_lambda_k2, attn_subln_g, attn_w_o, pre_ffn_g, post_ffn_g, ffn_w_in, ffn_w_dw, ffn_w_out, ple_pre_g, ple_w_gate, ple_b_gate, ple_w_proj, ple_post_g):
    batch, s, d = x.shape
    depth = p.shape[0]
    f = ffn_w_out.shape[1]
    t = _tiles(s, d, f)
    vec = lambda a: a.reshape(1, -1)
    p_all = p.reshape(depth * batch, s, p.shape[-1])
    outs = []
    for b in range(batch):
        h = x.reshape(s, d) if batch == 1 else x[b]
        for i in range(depth):
            j = i // 2
            layer_casts = [(ffn_w_in, i), (ffn_w_out, i), (ple_w_gate, i), (ple_w_proj, i)]
            if i % 2 == 0:
                u, w_pw2_bf = _norm_glu(h, vec(pre_mix_g[i]), conv_w_pw1[j], vec(conv_b_pw1[j]),
                                        [(conv_w_pw2, j)], t["proj_rows"], t["proj_cols"])
                h, w_in_bf, w_out_bf, w_gate_bf, w_proj_bf = _conv_mix(
                    u, h, conv_w_dw[j], vec(conv_b_dw[j]), vec(conv_ln_g[j]), vec(conv_ln_b[j]),
                    w_pw2_bf, vec(conv_b_pw2[j]), vec(post_mix_g[i]), layer_casts, t["conv_rows"])
            else:
                q_scale = HEAD_DIM ** -0.5 * math.log2(math.e)
                col_scale = jnp.concatenate([jnp.full((1, d), q_scale, F32),
                                             jnp.ones((1, 2 * d), F32)], axis=1)
                qkv, w_o_bf = _norm_matmul(h, vec(pre_mix_g[i]), attn_w_qkv[j], col_scale,
                                           [(attn_w_o, j)], t["proj_rows"], t["qkv_cols"], BF16)
                lambda_init = 0.8 - 0.6 * math.exp(-0.3 * i)
                o, w_in_bf, w_out_bf, w_gate_bf, w_proj_bf = _diff_attention(
                    qkv, vec(attn_lambda_q1[j]), vec(attn_lambda_k1[j]), vec(attn_lambda_q2[j]),
                    vec(attn_lambda_k2[j]), vec(attn_subln_g[j]), lambda_init,
                    layer_casts, t["attn_rows"])
                h = _proj_residual(o, h, w_o_bf, vec(post_mix_g[i]), t["out_rows"])
            h = _conv_ffn(h, vec(pre_ffn_g[i]), w_in_bf, ffn_w_dw, w_out_bf, i,
                          vec(post_ffn_g[i]), t["ffn_rows"], t["ffn_cols"])
            h = _per_layer_embed(h, p_all, i * batch + b, vec(ple_pre_g[i]), w_gate_bf,
                                 vec(ple_b_gate[i]), w_proj_bf, vec(ple_post_g[i]),
                                 t["ple_rows"])
        outs.append(h)
    return outs[0].reshape(1, s, d) if batch == 1 else jnp.stack(outs)
```

```python
import functools
import math

import jax
import jax.numpy as jnp
from jax import lax
from jax.experimental import pallas as pl
from jax.experimental.pallas import tpu as pltpu

F32 = jnp.float32
BF16 = jnp.bfloat16

RMS_EPS = 1e-6
LN_EPS = 1e-5
MASK_VALUE = -1e30
HEAD_DIM = 128
CONV_HALO = 32
SUBLANES = 8
LANES = 128
V7X_VMEM_LIMIT_BYTES = 56 * 1024 * 1024


def _params(n_axes, vmem_bytes=V7X_VMEM_LIMIT_BYTES):
    return pltpu.CompilerParams(dimension_semantics=("arbitrary",) * n_axes,
                                vmem_limit_bytes=vmem_bytes)


def _resident(shape):
    return pl.BlockSpec(shape, lambda *_: (0,) * len(shape), pipeline_mode=pl.Buffered(1))


def _cast_sidecar(w, layer, n_steps, step_of):
    rows, cols = w.shape[1:]
    bf16_rows = 2 * SUBLANES
    n_blocks = max(nb for nb in range(1, n_steps + 1)
                   if n_steps % nb == 0 and rows % nb == 0 and (rows // nb) % bf16_rows == 0)
    per, blk = n_steps // n_blocks, rows // n_blocks
    in_spec = pl.BlockSpec((None, blk, cols), lambda *g: (layer, step_of(*g) // per, 0))
    out_spec = pl.BlockSpec((blk, cols), lambda *g: (step_of(*g) // per, 0))
    return in_spec, out_spec, jax.ShapeDtypeStruct((rows, cols), BF16), per


def _host_call(body, casts, grid, in_specs, out_spec, out_shape, scratch_shapes, name, args):
    n_steps = math.prod(grid)

    def step_of(*g):
        step = 0
        for idx, extent in zip(g, grid):
            step = step * extent + idx
        return step

    sidecars = [_cast_sidecar(w, layer, n_steps, step_of) for w, layer in casts]
    n_in, n_cast = len(in_specs), len(casts)

    def kernel(*refs):
        ins, cast_ins = refs[:n_in], refs[n_in:n_in + n_cast]
        out, cast_outs = refs[n_in + n_cast], refs[n_in + n_cast + 1:n_in + 2 * n_cast + 1]
        scratch = refs[n_in + 2 * n_cast + 1:]
        step = step_of(*(pl.program_id(a) for a in range(len(grid))))
        for (_, _, _, per), src, dst in zip(sidecars, cast_ins, cast_outs):
            @pl.when(step % per == 0)
            def _():
                dst[...] = src[...].astype(dst.dtype)
        body(*ins, out, *scratch)

    return pl.pallas_call(
        kernel,
        grid=grid,
        in_specs=list(in_specs) + [sc[0] for sc in sidecars],
        out_specs=(out_spec,) + tuple(sc[1] for sc in sidecars),
        out_shape=(out_shape,) + tuple(sc[2] for sc in sidecars),
        scratch_shapes=scratch_shapes,
        compiler_params=_params(len(grid)),
        name=name,
    )(*args, *(w for w, _ in casts))


def _rms_rows(x, g):
    ms = jnp.mean(x * x, axis=-1, keepdims=True)
    return x * lax.rsqrt(ms + RMS_EPS) * g


def _sigmoid(x):
    return 1.0 / (1.0 + jnp.exp(-x))


def _dot(a, b):
    return jnp.dot(a, b, preferred_element_type=F32)


def _as_bf16(w):
    return w if w.dtype == BF16 else w.astype(BF16)


def _norm_to_scratch(dst_ref, src_ref, g, rows, dst_off=0, chunk=256):
    chunk = min(chunk, rows)

    def body(c, carry):
        r = pl.multiple_of(c * chunk, chunk)
        x = src_ref[pl.ds(r, chunk), :]
        dst_ref[pl.ds(dst_off + r, chunk), :] = _rms_rows(x, g).astype(dst_ref.dtype)
        return carry

    lax.fori_loop(0, rows // chunk, body, 0)


def _norm_glu_kernel(x_ref, g_ref, wa_ref, wg_ref, ba_ref, bg_ref, o_ref, xn_ref):
    @pl.when(pl.program_id(1) == 0)
    def _():
        _norm_to_scratch(xn_ref, x_ref, g_ref[...], x_ref.shape[0])

    xn = xn_ref[...]
    a = _dot(xn, _as_bf16(wa_ref[...])) + ba_ref[...]
    gate = _dot(xn, _as_bf16(wg_ref[...])) + bg_ref[...]
    o_ref[...] = (a * _sigmoid(gate)).astype(o_ref.dtype)


def _norm_glu(x, g, w, b, casts, tm, tn):
    s, d = x.shape
    n = w.shape[1] // 2
    nj = n // tn
    return _host_call(
        _norm_glu_kernel, casts,
        grid=(s // tm, nj),
        in_specs=[
            pl.BlockSpec((tm, d), lambda i, j: (i, 0)),
            pl.BlockSpec((1, d), lambda i, j: (0, 0)),
            pl.BlockSpec((d, tn), lambda i, j: (0, j)),
            pl.BlockSpec((d, tn), lambda i, j: (0, j + nj)),
            pl.BlockSpec((1, tn), lambda i, j: (0, j)),
            pl.BlockSpec((1, tn), lambda i, j: (0, j + nj)),
        ],
        out_spec=pl.BlockSpec((tm, tn), lambda i, j: (i, j)),
        out_shape=jax.ShapeDtypeStruct((s, n), F32),
        scratch_shapes=[pltpu.VMEM((tm, d), BF16)],
        name="norm_glu",
        args=(x, g, w, w, b, b))


def _norm_matmul_kernel(x_ref, g_ref, w_ref, cs_ref, o_ref, xn_ref):
    @pl.when(pl.program_id(1) == 0)
    def _():
        _norm_to_scratch(xn_ref, x_ref, g_ref[...], x_ref.shape[0])

    o_ref[...] = (_dot(xn_ref[...], _as_bf16(w_ref[...])) * cs_ref[...]).astype(o_ref.dtype)


def _norm_matmul(x, g, w, col_scale, casts, tm, tn, out_dtype):
    s, d = x.shape
    n = w.shape[1]
    return _host_call(
        _norm_matmul_kernel, casts,
        grid=(s // tm, n // tn),
        in_specs=[
            pl.BlockSpec((tm, d), lambda i, j: (i, 0)),
            pl.BlockSpec((1, d), lambda i, j: (0, 0)),
            pl.BlockSpec((d, tn), lambda i, j: (0, j)),
            pl.BlockSpec((1, tn), lambda i, j: (0, j)),
        ],
        out_spec=pl.BlockSpec((tm, tn), lambda i, j: (i, j)),
        out_shape=jax.ShapeDtypeStruct((s, n), out_dtype),
        scratch_shapes=[pltpu.VMEM((tm, d), BF16)],
        name="norm_matmul",
        args=(x, g, w, col_scale))


def _dwconv_window(win, w8_ref, lanes, rows, k):
    first = CONV_HALO - (k - 1)
    n_win = rows + CONV_HALO
    acc = None
    for b in range(SUBLANES):
        offs = [o for o in range(b, CONV_HALO + 1, SUBLANES) if 0 <= o - first < k]
        if not offs:
            continue
        shifted = win if b == 0 else pltpu.roll(win, n_win - b, axis=0)
        for o in offs:
            tap = pltpu.repeat(w8_ref[o - first, :, lanes], rows // SUBLANES, axis=0)
            term = shifted[o - b:o - b + rows] * tap
            acc = term if acc is None else acc + term
    return acc


def _conv_mix_kernel(u_ref, halo_ref, h_ref, wdw_ref, bdw_ref, lng_ref, lnb_ref, w2_ref, b2_ref,
                     pg_ref, o_ref, ext_ref, act_ref, w8_ref, *, k, rows):
    tm, d = u_ref.shape
    ext_ref[0:CONV_HALO, :] = jnp.where(pl.program_id(0) > 0, halo_ref[...], 0.0)
    ext_ref[CONV_HALO:, :] = u_ref[...]
    for j in range(k):
        w8_ref[j] = jnp.broadcast_to(wdw_ref[j:j + 1, :], (SUBLANES, d))

    n_lane = d // LANES

    def conv_body(c, carry):
        r0 = pl.multiple_of((c // n_lane) * rows, rows)
        lanes = pl.ds(pl.multiple_of((c % n_lane) * LANES, LANES), LANES)
        win = ext_ref[pl.ds(r0, rows + CONV_HALO), lanes]
        y = _dwconv_window(win, w8_ref, lanes, rows, k) + bdw_ref[:, lanes]
        ext_ref[pl.ds(r0, rows), lanes] = y
        return carry

    lax.fori_loop(0, (tm // rows) * n_lane, conv_body, 0)

    chunk = min(256, tm)

    def ln_body(c, carry):
        r = pl.multiple_of(c * chunk, chunk)
        x = ext_ref[pl.ds(r, chunk), :]
        mu = jnp.mean(x, axis=-1, keepdims=True)
        xc = x - mu
        y = xc * lax.rsqrt(jnp.mean(xc * xc, axis=-1, keepdims=True) + LN_EPS)
        y = y * lng_ref[...] + lnb_ref[...]
        act_ref[pl.ds(r, chunk), :] = (y * _sigmoid(y)).astype(act_ref.dtype)
        return carry

    lax.fori_loop(0, tm // chunk, ln_body, 0)

    y = _dot(act_ref[...], w2_ref[...]) + b2_ref[...]
    o_ref[...] = h_ref[...] + _rms_rows(y, pg_ref[...])


def _conv_mix(u, h, w_dw, b_dw, ln_g, ln_b, w2, b2, post_g, casts, tm):
    s, d = u.shape
    k = w_dw.shape[0]
    halo_blocks = tm // CONV_HALO
    row = lambda i: (i, 0)
    return _host_call(
        functools.partial(_conv_mix_kernel, k=k, rows=min(128, tm)), casts,
        grid=(s // tm,),
        in_specs=[
            pl.BlockSpec((tm, d), row),
            pl.BlockSpec((CONV_HALO, d), lambda i: (jnp.maximum(i * halo_blocks - 1, 0), 0)),
            pl.BlockSpec((tm, d), row),
            _resident((k, d)),
            _resident((1, d)),
            _resident((1, d)),
            _resident((1, d)),
            _resident((d, d)),
            _resident((1, d)),
            _resident((1, d)),
        ],
        out_spec=pl.BlockSpec((tm, d), row),
        out_shape=jax.ShapeDtypeStruct((s, d), F32),
        scratch_shapes=[pltpu.VMEM((tm + CONV_HALO, d), F32), pltpu.VMEM((tm, d), BF16),
                        pltpu.VMEM((k, SUBLANES, d), F32)],
        name="conv_mix",
        args=(u, u, h, w_dw, b_dw, ln_g, ln_b, w2, b2, post_g))


def _causal3(u, tail, w):
    tm = u.shape[0]
    ext = jnp.concatenate([tail, u], axis=0)
    return (u * w[2:3]
            + ext[SUBLANES - 1:SUBLANES - 1 + tm] * w[1:2]
            + ext[SUBLANES - 2:SUBLANES - 2 + tm] * w[0:1])


def _ffn_kernel(h_ref, g_ref, wg_ref, wu_ref, dwg_ref, dwu_ref, wo_ref, pg_ref,
                o_ref, xn_ref, tail_ref):
    i = pl.program_id(0)
    j = pl.program_id(1)
    tm = h_ref.shape[0]

    @pl.when(j == 0)
    def _():
        _norm_to_scratch(xn_ref, h_ref, g_ref[...], tm)
        o_ref[...] = jnp.zeros(o_ref.shape, o_ref.dtype)

    @pl.when(i == 0)
    def _():
        tail_ref[j] = jnp.zeros(tail_ref.shape[1:], F32)

    xn = xn_ref[...]
    pre_gate = _dot(xn, wg_ref[...])
    pre_up = _dot(xn, wu_ref[...])
    gate = _causal3(pre_gate, tail_ref[j, 0], dwg_ref[...])
    up = _causal3(pre_up, tail_ref[j, 1], dwu_ref[...])
    tail_ref[j, 0] = pre_gate[tm - SUBLANES:]
    tail_ref[j, 1] = pre_up[tm - SUBLANES:]
    act = (gate * _sigmoid(gate) * up).astype(BF16)
    o_ref[...] += _dot(act, wo_ref[...])

    @pl.when(j == pl.num_programs(1) - 1)
    def _():
        chunk = min(256, tm)

        def body(c, carry):
            r = pl.multiple_of(c * chunk, chunk)
            rows = pl.ds(r, chunk)
            o_ref[rows, :] = h_ref[rows, :] + _rms_rows(o_ref[rows, :], pg_ref[...])
            return carry

        lax.fori_loop(0, tm // chunk, body, 0)


def _conv_ffn(h, pre_g, w_in, w_dw, w_out, layer, post_g, tm, tf):
    s, d = h.shape
    f = w_out.shape[0]
    nj = f // tf
    return pl.pallas_call(
        _ffn_kernel,
        grid=(s // tm, nj),
        in_specs=[
            pl.BlockSpec((tm, d), lambda i, j: (i, 0), pipeline_mode=pl.Buffered(1)),
            pl.BlockSpec((1, d), lambda i, j: (0, 0)),
            pl.BlockSpec((d, tf), lambda i, j: (0, j)),
            pl.BlockSpec((d, tf), lambda i, j: (0, j + nj)),
            pl.BlockSpec((None, 3, tf), lambda i, j: (layer, 0, j)),
            pl.BlockSpec((None, 3, tf), lambda i, j: (layer, 0, j + nj)),
            pl.BlockSpec((tf, d), lambda i, j: (j, 0)),
            pl.BlockSpec((1, d), lambda i, j: (0, 0)),
        ],
        out_specs=pl.BlockSpec((tm, d), lambda i, j: (i, 0)),
        out_shape=jax.ShapeDtypeStruct((s, d), F32),
        scratch_shapes=[pltpu.VMEM((tm, d), BF16), pltpu.VMEM((nj, 2, SUBLANES, tf), F32)],
        compiler_params=_params(2),
        name="conv_ffn",
    )(h, pre_g, w_in, w_in, w_dw, w_dw, w_out, post_g)


def _ple_kernel(h_ref, p_ref, g_ref, wg_ref, bg_ref, wp_ref, pg_ref, o_ref):
    h = h_ref[...]
    xn = _rms_rows(h, g_ref[...]).astype(BF16)
    gate = _sigmoid(_dot(xn, wg_ref[...]) + bg_ref[...])
    emb = _dot(p_ref[...].astype(BF16), wp_ref[...])
    o_ref[...] = h + _rms_rows(gate * emb, pg_ref[...])


def _per_layer_embed(h, p, p_index, pre_g, w_gate, b_gate, w_proj, post_g, casts, tm):
    s, d = h.shape
    e = p.shape[2]
    row = lambda i: (i, 0)
    return _host_call(
        _ple_kernel, casts,
        grid=(s // tm,),
        in_specs=[
            pl.BlockSpec((tm, d), row),
            pl.BlockSpec((None, tm, e), lambda i: (p_index, i, 0)),
            _resident((1, d)),
            _resident((d, d)),
            _resident((1, d)),
            _resident((e, d)),
            _resident((1, d)),
        ],
        out_spec=pl.BlockSpec((tm, d), row),
        out_shape=jax.ShapeDtypeStruct((s, d), F32),
        scratch_shapes=[],
        name="per_layer_embed",
        args=(h, p, pre_g, w_gate, b_gate, w_proj, post_g))


L_FLOOR = 2.0 ** -64
TILES_PER_TRIP = 4


def _diff_attn_kernel(q_ref, k_ref, v_ref, lq1_ref, lk1_ref, lq2_ref, lk2_ref, sg_ref, o_ref,
                      b_ref, l_ref, acc_ref, kn_ref, *, lambda_init):
    qi = pl.program_id(1)
    tq = q_ref.shape[0]
    tk = tq
    n_lane_tiles = tk // LANES
    n_keys = k_ref.shape[0]

    def map_lanes(c):
        return slice(c * HEAD_DIM, (c + 1) * HEAD_DIM)

    def scores(k0, c, diagonal):
        s = lax.dot_general(q_ref[:, map_lanes(c)], k_ref[pl.ds(k0, tk), map_lanes(c)],
                            (((1,), (1,)), ((), ())), preferred_element_type=F32)
        if diagonal:
            rows = lax.broadcasted_iota(jnp.int32, s.shape, 0)
            cols = lax.broadcasted_iota(jnp.int32, s.shape, 1)
            s = jnp.where(rows >= cols, s, MASK_VALUE)
        return s

    @pl.when(qi == 0)
    def _():
        chunk = min(512, n_keys)
        for c in range(2):
            def body(i, best):
                rows = pl.ds(pl.multiple_of(i * chunk, chunk), chunk)
                kk = k_ref[rows, map_lanes(c)].astype(F32)
                return jnp.maximum(best, jnp.sum(kk * kk, axis=1, keepdims=True))

            best = lax.fori_loop(0, n_keys // chunk, body, jnp.zeros((chunk, 1), F32))
            kn_ref[c] = jnp.broadcast_to(jnp.sqrt(jnp.max(best, axis=0, keepdims=True)),
                                         (SUBLANES, LANES))

    def block(row0, n_rows, k0, n_keys_blk, triangle):
        rows = slice(row0, row0 + n_rows)
        keys = pl.ds(k0, n_keys_blk)
        v = v_ref[keys, :]
        for c in range(2):
            s = lax.dot_general(q_ref[rows, map_lanes(c)], k_ref[keys, map_lanes(c)],
                                (((1,), (1,)), ((), ())), preferred_element_type=F32)
            if triangle:
                r_idx = lax.broadcasted_iota(jnp.int32, s.shape, 0)
                c_idx = lax.broadcasted_iota(jnp.int32, s.shape, 1)
                s = jnp.where(r_idx >= c_idx, s, MASK_VALUE)
            p = jnp.exp2(s - pltpu.repeat(b_ref[c, rows, :], n_keys_blk // LANES, axis=1))
            psum = p[:, 0:LANES]
            for t in range(1, n_keys_blk // LANES):
                psum = psum + p[:, t * LANES:(t + 1) * LANES]
            l_ref[c, rows, :] += psum
            acc_ref[c, rows, :] += _dot(p.astype(BF16), v)

    def tile(k0, diagonal):
        if not diagonal:
            block(0, tq, k0, tk, False)
        else:
            half = tk // 2
            block(0, tq, k0, half, True)
            block(half, tq - half, pl.multiple_of(k0 + half, half), half, True)

    def start(t):
        return pl.multiple_of(t * tk, tk)

    def sweep():
        l_ref[...] = jnp.zeros(l_ref.shape, F32)
        acc_ref[...] = jnp.zeros(acc_ref.shape, F32)

        def trip(g, carry):
            for u in range(TILES_PER_TRIP):
                tile(start(TILES_PER_TRIP * g + u), False)
            return carry

        lax.fori_loop(0, qi // TILES_PER_TRIP, trip, 0)
        base = (qi // TILES_PER_TRIP) * TILES_PER_TRIP
        rem = qi % TILES_PER_TRIP

        @pl.when(rem >= 2)
        def _():
            tile(start(base), False)
            tile(start(base + 1), False)

        last = base + jnp.where(rem >= 2, 2, 0)

        @pl.when(rem % 2 == 1)
        def _():
            tile(start(last), False)
            tile(start(qi), True)

        @pl.when(rem % 2 == 0)
        def _():
            tile(start(qi), True)

    def exact_row_max():
        for c in range(2):
            def fold(best, s):
                for u in range(n_lane_tiles):
                    best = jnp.maximum(best, s[:, u * LANES:(u + 1) * LANES])
                return best

            best = lax.fori_loop(0, qi, lambda t, best: fold(best, scores(start(t), c, False)),
                                 jnp.full((tq, LANES), MASK_VALUE, F32))
            best = fold(best, scores(start(qi), c, True))
            b_ref[c] = jnp.broadcast_to(jnp.max(best, axis=1, keepdims=True), (tq, LANES))

    for c in range(2):
        qq = q_ref[:, map_lanes(c)].astype(F32)
        b_ref[c] = jnp.sqrt(jnp.sum(qq * qq, axis=1, keepdims=True)) * kn_ref[c, 0:1, :]
    sweep()

    l_min = jnp.min(jnp.minimum(jnp.sum(l_ref[0], axis=1, keepdims=True),
                                jnp.sum(l_ref[1], axis=1, keepdims=True)))

    @pl.when(jnp.logical_not(l_min >= L_FLOOR))
    def _():
        exact_row_max()
        sweep()

    lam = (jnp.exp(jnp.sum(lq1_ref[...] * lk1_ref[...], keepdims=True))
           - jnp.exp(jnp.sum(lq2_ref[...] * lk2_ref[...], keepdims=True)) + lambda_init)
    l0 = jnp.sum(l_ref[0], axis=1, keepdims=True)
    l1 = jnp.sum(l_ref[1], axis=1, keepdims=True)
    o = acc_ref[0] / l0 - lam * (acc_ref[1] / l1)
    o_ref[...] = (_rms_rows(o, sg_ref[...]) * (1.0 - lambda_init)).astype(o_ref.dtype)


def _diff_attention(qkv, lq1, lk1, lq2, lk2, subln_g, lambda_init, casts, tq):
    s = qkv.shape[0]
    vd = 2 * HEAD_DIM
    heads = qkv.shape[1] // (3 * vd)
    vec = pl.BlockSpec((1, HEAD_DIM), lambda h, i: (0, 0))
    return _host_call(
        functools.partial(_diff_attn_kernel, lambda_init=lambda_init), casts,
        grid=(heads, s // tq),
        in_specs=[
            pl.BlockSpec((tq, vd), lambda h, i: (i, h)),
            pl.BlockSpec((s, vd), lambda h, i: (0, heads + h)),
            pl.BlockSpec((s, vd), lambda h, i: (0, 2 * heads + h)),
            vec, vec, vec, vec,
            pl.BlockSpec((1, vd), lambda h, i: (0, 0)),
        ],
        out_spec=pl.BlockSpec((tq, vd), lambda h, i: (i, h)),
        out_shape=jax.ShapeDtypeStruct((s, heads * vd), BF16),
        scratch_shapes=[pltpu.VMEM((2, tq, LANES), F32), pltpu.VMEM((2, tq, LANES), F32),
                        pltpu.VMEM((2, tq, vd), F32), pltpu.VMEM((2, SUBLANES, LANES), F32)],
        name="diff_attention",
        args=(qkv, qkv, qkv, lq1, lk1, lq2, lk2, subln_g))


def _proj_residual_kernel(a_ref, h_ref, w_ref, pg_ref, o_ref):
    o_ref[...] = h_ref[...] + _rms_rows(_dot(a_ref[...], w_ref[...]), pg_ref[...])


def _proj_residual(a, h, w, post_g, tm):
    s, d = h.shape
    row = lambda i: (i, 0)
    return pl.pallas_call(
        _proj_residual_kernel,
        grid=(s // tm,),
        in_specs=[pl.BlockSpec((tm, a.shape[1]), row), pl.BlockSpec((tm, d), row),
                  _resident(w.shape), _resident((1, d))],
        out_specs=pl.BlockSpec((tm, d), row),
        out_shape=jax.ShapeDtypeStruct((s, d), F32),
        compiler_params=_params(1),
        name="proj_residual",
    )(a, h, w, post_g)


def _tiles(s, d, f):
    ffn_cols = 512 if f % 512 == 0 else 256
    return dict(
        proj_rows=min(1024, s), proj_cols=min(512, d), qkv_cols=min(1024, d),
        conv_rows=min(256, s),
        ffn_rows=min(1024, s), ffn_cols=min(ffn_cols, f),
        ple_rows=min(512, s),
        attn_rows=min(1024, s),
        out_rows=min(512, s),
    )


def kernel(x, p, pre_mix_g, post_mix_g, conv_w_pw1, conv_b_pw1, conv_w_dw, conv_b_dw, conv_ln_g, conv_ln_b, conv_w_pw2, conv_b_pw2, attn_w_qkv, attn_lambda_q1, attn_lambda_k1, attn_lambda_q2, attn_lambda_k2, attn_subln_g, attn_w_o, pre_ffn_g, post_ffn_g, ffn_w_in, ffn_w_dw, ffn_w_out, ple_pre_g, ple_w_gate, ple_b_gate, ple_w_proj, ple_post_g):
    batch, s, d = x.shape
    depth = p.shape[0]
    f = ffn_w_out.shape[1]
    t = _tiles(s, d, f)
    vec = lambda a: a.reshape(1, -1)
    p_all = p.reshape(depth * batch, s, p.shape[-1])
    first_proj = lambda i: (conv_w_pw1, i // 2) if i % 2 == 0 else (attn_w_qkv, i // 2)
    outs = []
    for b in range(batch):
        h = x.reshape(s, d) if batch == 1 else x[b]
        w_first = first_proj(0)[0][0]
        for i in range(depth):
            j = i // 2
            layer_casts = [(ffn_w_in, i), (ffn_w_out, i), (ple_w_gate, i), (ple_w_proj, i)]
            if i % 2 == 0:
                u, w_pw2_bf = _norm_glu(h, vec(pre_mix_g[i]), w_first, vec(conv_b_pw1[j]),
                                        [(conv_w_pw2, j)], t["proj_rows"], t["proj_cols"])
                h, w_in_bf, w_out_bf, w_gate_bf, w_proj_bf = _conv_mix(
                    u, h, conv_w_dw[j], vec(conv_b_dw[j]), vec(conv_ln_g[j]), vec(conv_ln_b[j]),
                    w_pw2_bf, vec(conv_b_pw2[j]), vec(post_mix_g[i]), layer_casts, t["conv_rows"])
            else:
                q_scale = HEAD_DIM ** -0.5 * math.log2(math.e)
                col_scale = jnp.concatenate([jnp.full((1, d), q_scale, F32),
                                             jnp.ones((1, 2 * d), F32)], axis=1)
                qkv, w_o_bf = _norm_matmul(h, vec(pre_mix_g[i]), w_first, col_scale,
                                           [(attn_w_o, j)], t["proj_rows"], t["qkv_cols"], BF16)
                lambda_init = 0.8 - 0.6 * math.exp(-0.3 * i)
                o, w_in_bf, w_out_bf, w_gate_bf, w_proj_bf = _diff_attention(
                    qkv, vec(attn_lambda_q1[j]), vec(attn_lambda_k1[j]), vec(attn_lambda_q2[j]),
                    vec(attn_lambda_k2[j]), vec(attn_subln_g[j]), lambda_init,
                    layer_casts, t["attn_rows"])
                h = _proj_residual(o, h, w_o_bf, vec(post_mix_g[i]), t["out_rows"])
            h = _conv_ffn(h, vec(pre_ffn_g[i]), w_in_bf, ffn_w_dw, w_out_bf, i,
                          vec(post_ffn_g[i]), t["ffn_rows"], t["ffn_cols"])
            next_casts = [first_proj(i + 1)] if i + 1 < depth else []
            h, *w_next = _per_layer_embed(h, p_all, i * batch + b, vec(ple_pre_g[i]), w_gate_bf,
                                          vec(ple_b_gate[i]), w_proj_bf, vec(ple_post_g[i]),
                                          next_casts, t["ple_rows"])
            w_first = w_next[0] if w_next else None
        outs.append(h)
    return outs[0].reshape(1, s, d) if batch == 1 else jnp.stack(outs)
```

```python
import functools
import math

import jax
import jax.numpy as jnp
from jax import lax
from jax.experimental import pallas as pl
from jax.experimental.pallas import tpu as pltpu

F32 = jnp.float32
BF16 = jnp.bfloat16

RMS_EPS = 1e-6
LN_EPS = 1e-5
MASK_VALUE = -1e30
HEAD_DIM = 128
CONV_HALO = 32
SUBLANES = 8
LANES = 128
V7X_VMEM_LIMIT_BYTES = 56 * 1024 * 1024


def _params(n_axes, vmem_bytes=V7X_VMEM_LIMIT_BYTES):
    return pltpu.CompilerParams(dimension_semantics=("arbitrary",) * n_axes,
                                vmem_limit_bytes=vmem_bytes)


def _resident(shape):
    return pl.BlockSpec(shape, lambda *_: (0,) * len(shape), pipeline_mode=pl.Buffered(1))


def _cast_sidecar(w, layer, n_steps, step_of):
    rows, cols = w.shape[1:]
    bf16_rows = 2 * SUBLANES
    n_blocks = max(nb for nb in range(1, n_steps + 1)
                   if n_steps % nb == 0 and rows % nb == 0 and (rows // nb) % bf16_rows == 0)
    per, blk = n_steps // n_blocks, rows // n_blocks
    in_spec = pl.BlockSpec((None, blk, cols), lambda *g: (layer, step_of(*g) // per, 0))
    out_spec = pl.BlockSpec((blk, cols), lambda *g: (step_of(*g) // per, 0))
    return in_spec, out_spec, jax.ShapeDtypeStruct((rows, cols), BF16), per


def _host_call(body, casts, grid, in_specs, out_spec, out_shape, scratch_shapes, name, args):
    n_steps = math.prod(grid)

    def step_of(*g):
        step = 0
        for idx, extent in zip(g, grid):
            step = step * extent + idx
        return step

    sidecars = [_cast_sidecar(w, layer, n_steps, step_of) for w, layer in casts]
    n_in, n_cast = len(in_specs), len(casts)

    def kernel(*refs):
        ins, cast_ins = refs[:n_in], refs[n_in:n_in + n_cast]
        out, cast_outs = refs[n_in + n_cast], refs[n_in + n_cast + 1:n_in + 2 * n_cast + 1]
        scratch = refs[n_in + 2 * n_cast + 1:]
        step = step_of(*(pl.program_id(a) for a in range(len(grid))))
        for (_, _, _, per), src, dst in zip(sidecars, cast_ins, cast_outs):
            @pl.when(step % per == 0)
            def _():
                dst[...] = src[...].astype(dst.dtype)
        body(*ins, out, *scratch)

    return pl.pallas_call(
        kernel,
        grid=grid,
        in_specs=list(in_specs) + [sc[0] for sc in sidecars],
        out_specs=(out_spec,) + tuple(sc[1] for sc in sidecars),
        out_shape=(out_shape,) + tuple(sc[2] for sc in sidecars),
        scratch_shapes=scratch_shapes,
        compiler_params=_params(len(grid)),
        name=name,
    )(*args, *(w for w, _ in casts))


def _rms_rows(x, g):
    ms = jnp.mean(x * x, axis=-1, keepdims=True)
    return x * lax.rsqrt(ms + RMS_EPS) * g


def _sigmoid(x):
    return 1.0 / (1.0 + jnp.exp(-x))


def _dot(a, b):
    return jnp.dot(a, b, preferred_element_type=F32)


def _as_bf16(w):
    return w if w.dtype == BF16 else w.astype(BF16)


def _norm_to_scratch(dst_ref, src_ref, g, rows, dst_off=0, chunk=256):
    chunk = min(chunk, rows)

    def body(c, carry):
        r = pl.multiple_of(c * chunk, chunk)
        x = src_ref[pl.ds(r, chunk), :]
        dst_ref[pl.ds(dst_off + r, chunk), :] = _rms_rows(x, g).astype(dst_ref.dtype)
        return carry

    lax.fori_loop(0, rows // chunk, body, 0)


def _norm_glu_kernel(x_ref, g_ref, wa_ref, wg_ref, ba_ref, bg_ref, o_ref, xn_ref):
    @pl.when(pl.program_id(1) == 0)
    def _():
        _norm_to_scratch(xn_ref, x_ref, g_ref[...], x_ref.shape[0])

    xn = xn_ref[...]
    a = _dot(xn, _as_bf16(wa_ref[...])) + ba_ref[...]
    gate = _dot(xn, _as_bf16(wg_ref[...])) + bg_ref[...]
    o_ref[...] = (a * _sigmoid(gate)).astype(o_ref.dtype)


def _norm_glu(x, g, w, b, casts, tm, tn):
    s, d = x.shape
    n = w.shape[1] // 2
    nj = n // tn
    return _host_call(
        _norm_glu_kernel, casts,
        grid=(s // tm, nj),
        in_specs=[
            pl.BlockSpec((tm, d), lambda i, j: (i, 0)),
            pl.BlockSpec((1, d), lambda i, j: (0, 0)),
            pl.BlockSpec((d, tn), lambda i, j: (0, j)),
            pl.BlockSpec((d, tn), lambda i, j: (0, j + nj)),
            pl.BlockSpec((1, tn), lambda i, j: (0, j)),
            pl.BlockSpec((1, tn), lambda i, j: (0, j + nj)),
        ],
        out_spec=pl.BlockSpec((tm, tn), lambda i, j: (i, j)),
        out_shape=jax.ShapeDtypeStruct((s, n), F32),
        scratch_shapes=[pltpu.VMEM((tm, d), BF16)],
        name="norm_glu",
        args=(x, g, w, w, b, b))


def _norm_matmul_kernel(x_ref, g_ref, w_ref, cs_ref, o_ref, xn_ref):
    @pl.when(pl.program_id(1) == 0)
    def _():
        _norm_to_scratch(xn_ref, x_ref, g_ref[...], x_ref.shape[0])

    o_ref[...] = (_dot(xn_ref[...], _as_bf16(w_ref[...])) * cs_ref[...]).astype(o_ref.dtype)


def _norm_matmul(x, g, w, col_scale, casts, tm, tn, out_dtype):
    s, d = x.shape
    n = w.shape[1]
    return _host_call(
        _norm_matmul_kernel, casts,
        grid=(s // tm, n // tn),
        in_specs=[
            pl.BlockSpec((tm, d), lambda i, j: (i, 0)),
            pl.BlockSpec((1, d), lambda i, j: (0, 0)),
            pl.BlockSpec((d, tn), lambda i, j: (0, j)),
            pl.BlockSpec((1, tn), lambda i, j: (0, j)),
        ],
        out_spec=pl.BlockSpec((tm, tn), lambda i, j: (i, j)),
        out_shape=jax.ShapeDtypeStruct((s, n), out_dtype),
        scratch_shapes=[pltpu.VMEM((tm, d), BF16)],
        name="norm_matmul",
        args=(x, g, w, col_scale))


def _dwconv_window(win, w8_ref, lanes, rows, k):
    first = CONV_HALO - (k - 1)
    n_win = rows + CONV_HALO
    acc = None
    for b in range(SUBLANES):
        offs = [o for o in range(b, CONV_HALO + 1, SUBLANES) if 0 <= o - first < k]
        if not offs:
            continue
        shifted = win if b == 0 else pltpu.roll(win, n_win - b, axis=0)
        for o in offs:
            tap = pltpu.repeat(w8_ref[o - first, :, lanes], rows // SUBLANES, axis=0)
            term = shifted[o - b:o - b + rows] * tap
            acc = term if acc is None else acc + term
    return acc


def _conv_mix_kernel(u_ref, halo_ref, h_ref, wdw_ref, bdw_ref, lng_ref, lnb_ref, w2_ref, b2_ref,
                     pg_ref, o_ref, ext_ref, act_ref, w8_ref, *, k, rows):
    tm, d = u_ref.shape
    ext_ref[0:CONV_HALO, :] = jnp.where(pl.program_id(0) > 0, halo_ref[...], 0.0)
    ext_ref[CONV_HALO:, :] = u_ref[...]
    for j in range(k):
        w8_ref[j] = jnp.broadcast_to(wdw_ref[j:j + 1, :], (SUBLANES, d))

    n_lane = d // LANES

    def conv_body(c, carry):
        r0 = pl.multiple_of((c // n_lane) * rows, rows)
        lanes = pl.ds(pl.multiple_of((c % n_lane) * LANES, LANES), LANES)
        win = ext_ref[pl.ds(r0, rows + CONV_HALO), lanes]
        y = _dwconv_window(win, w8_ref, lanes, rows, k) + bdw_ref[:, lanes]
        ext_ref[pl.ds(r0, rows), lanes] = y
        return carry

    lax.fori_loop(0, (tm // rows) * n_lane, conv_body, 0)

    chunk = min(256, tm)

    def ln_body(c, carry):
        r = pl.multiple_of(c * chunk, chunk)
        x = ext_ref[pl.ds(r, chunk), :]
        mu = jnp.mean(x, axis=-1, keepdims=True)
        xc = x - mu
        y = xc * lax.rsqrt(jnp.mean(xc * xc, axis=-1, keepdims=True) + LN_EPS)
        y = y * lng_ref[...] + lnb_ref[...]
        act_ref[pl.ds(r, chunk), :] = (y * _sigmoid(y)).astype(act_ref.dtype)
        return carry

    lax.fori_loop(0, tm // chunk, ln_body, 0)

    y = _dot(act_ref[...], w2_ref[...]) + b2_ref[...]
    o_ref[...] = h_ref[...] + _rms_rows(y, pg_ref[...])


def _conv_mix(u, h, w_dw, b_dw, ln_g, ln_b, w2, b2, post_g, casts, tm):
    s, d = u.shape
    k = w_dw.shape[0]
    halo_blocks = tm // CONV_HALO
    row = lambda i: (i, 0)
    return _host_call(
        functools.partial(_conv_mix_kernel, k=k, rows=min(128, tm)), casts,
        grid=(s // tm,),
        in_specs=[
            pl.BlockSpec((tm, d), row),
            pl.BlockSpec((CONV_HALO, d), lambda i: (jnp.maximum(i * halo_blocks - 1, 0), 0)),
            pl.BlockSpec((tm, d), row),
            _resident((k, d)),
            _resident((1, d)),
            _resident((1, d)),
            _resident((1, d)),
            _resident((d, d)),
            _resident((1, d)),
            _resident((1, d)),
        ],
        out_spec=pl.BlockSpec((tm, d), row),
        out_shape=jax.ShapeDtypeStruct((s, d), F32),
        scratch_shapes=[pltpu.VMEM((tm + CONV_HALO, d), F32), pltpu.VMEM((tm, d), BF16),
                        pltpu.VMEM((k, SUBLANES, d), F32)],
        name="conv_mix",
        args=(u, u, h, w_dw, b_dw, ln_g, ln_b, w2, b2, post_g))


def _causal3(u, tail, w):
    tm = u.shape[0]
    ext = jnp.concatenate([tail, u], axis=0)
    return (u * w[2:3]
            + ext[SUBLANES - 1:SUBLANES - 1 + tm] * w[1:2]
            + ext[SUBLANES - 2:SUBLANES - 2 + tm] * w[0:1])


def _ffn_kernel(h_ref, g_ref, wg_ref, wu_ref, dwg_ref, dwu_ref, wo_ref, pg_ref,
                o_ref, xn_ref, tail_ref):
    i = pl.program_id(0)
    j = pl.program_id(1)
    tm = h_ref.shape[0]

    @pl.when(j == 0)
    def _():
        _norm_to_scratch(xn_ref, h_ref, g_ref[...], tm)
        o_ref[...] = jnp.zeros(o_ref.shape, o_ref.dtype)

    @pl.when(i == 0)
    def _():
        tail_ref[j] = jnp.zeros(tail_ref.shape[1:], F32)

    xn = xn_ref[...]
    pre_gate = _dot(xn, wg_ref[...])
    pre_up = _dot(xn, wu_ref[...])
    gate = _causal3(pre_gate, tail_ref[j, 0], dwg_ref[...])
    up = _causal3(pre_up, tail_ref[j, 1], dwu_ref[...])
    tail_ref[j, 0] = pre_gate[tm - SUBLANES:]
    tail_ref[j, 1] = pre_up[tm - SUBLANES:]
    act = (gate * _sigmoid(gate) * up).astype(BF16)
    o_ref[...] += _dot(act, wo_ref[...])

    @pl.when(j == pl.num_programs(1) - 1)
    def _():
        chunk = min(256, tm)

        def body(c, carry):
            r = pl.multiple_of(c * chunk, chunk)
            rows = pl.ds(r, chunk)
            o_ref[rows, :] = h_ref[rows, :] + _rms_rows(o_ref[rows, :], pg_ref[...])
            return carry

        lax.fori_loop(0, tm // chunk, body, 0)


def _conv_ffn(h, pre_g, w_in, w_dw, w_out, layer, post_g, tm, tf):
    s, d = h.shape
    f = w_out.shape[0]
    nj = f // tf
    return pl.pallas_call(
        _ffn_kernel,
        grid=(s // tm, nj),
        in_specs=[
            pl.BlockSpec((tm, d), lambda i, j: (i, 0), pipeline_mode=pl.Buffered(1)),
            pl.BlockSpec((1, d), lambda i, j: (0, 0)),
            pl.BlockSpec((d, tf), lambda i, j: (0, j)),
            pl.BlockSpec((d, tf), lambda i, j: (0, j + nj)),
            pl.BlockSpec((None, 3, tf), lambda i, j: (layer, 0, j)),
            pl.BlockSpec((None, 3, tf), lambda i, j: (layer, 0, j + nj)),
            pl.BlockSpec((tf, d), lambda i, j: (j, 0)),
            pl.BlockSpec((1, d), lambda i, j: (0, 0)),
        ],
        out_specs=pl.BlockSpec((tm, d), lambda i, j: (i, 0)),
        out_shape=jax.ShapeDtypeStruct((s, d), F32),
        scratch_shapes=[pltpu.VMEM((tm, d), BF16), pltpu.VMEM((nj, 2, SUBLANES, tf), F32)],
        compiler_params=_params(2),
        name="conv_ffn",
    )(h, pre_g, w_in, w_in, w_dw, w_dw, w_out, post_g)


def _ple_kernel(h_ref, p_ref, g_ref, wg_ref, bg_ref, wp_ref, pg_ref, o_ref):
    h = h_ref[...]
    xn = _rms_rows(h, g_ref[...]).astype(BF16)
    gate = _sigmoid(_dot(xn, wg_ref[...]) + bg_ref[...])
    emb = _dot(p_ref[...].astype(BF16), wp_ref[...])
    o_ref[...] = h + _rms_rows(gate * emb, pg_ref[...])


def _per_layer_embed(h, p, p_index, pre_g, w_gate, b_gate, w_proj, post_g, casts, tm):
    s, d = h.shape
    e = p.shape[2]
    row = lambda i: (i, 0)
    return _host_call(
        _ple_kernel, casts,
        grid=(s // tm,),
        in_specs=[
            pl.BlockSpec((tm, d), row),
            pl.BlockSpec((None, tm, e), lambda i: (p_index, i, 0)),
            _resident((1, d)),
            _resident((d, d)),
            _resident((1, d)),
            _resident((e, d)),
            _resident((1, d)),
        ],
        out_spec=pl.BlockSpec((tm, d), row),
        out_shape=jax.ShapeDtypeStruct((s, d), F32),
        scratch_shapes=[],
        name="per_layer_embed",
        args=(h, p, pre_g, w_gate, b_gate, w_proj, post_g))


L_FLOOR = 2.0 ** -64
TILES_PER_TRIP = 4


def _diff_attn_kernel(q_ref, k_ref, v_ref, lq1_ref, lk1_ref, lq2_ref, lk2_ref, sg_ref, o_ref,
                      b_ref, l_ref, acc_ref, kn_ref, *, lambda_init):
    qi = pl.program_id(1)
    tq = q_ref.shape[0]
    tk = tq
    n_lane_tiles = tk // LANES
    n_keys = k_ref.shape[0]

    def map_lanes(c):
        return slice(c * HEAD_DIM, (c + 1) * HEAD_DIM)

    def scores(k0, c, diagonal):
        s = lax.dot_general(q_ref[:, map_lanes(c)], k_ref[pl.ds(k0, tk), map_lanes(c)],
                            (((1,), (1,)), ((), ())), preferred_element_type=F32)
        if diagonal:
            rows = lax.broadcasted_iota(jnp.int32, s.shape, 0)
            cols = lax.broadcasted_iota(jnp.int32, s.shape, 1)
            s = jnp.where(rows >= cols, s, MASK_VALUE)
        return s

    @pl.when(qi == 0)
    def _():
        chunk = min(512, n_keys)
        for c in range(2):
            def body(i, best):
                rows = pl.ds(pl.multiple_of(i * chunk, chunk), chunk)
                kk = k_ref[rows, map_lanes(c)].astype(F32)
                return jnp.maximum(best, jnp.sum(kk * kk, axis=1, keepdims=True))

            best = lax.fori_loop(0, n_keys // chunk, body, jnp.zeros((chunk, 1), F32))
            kn_ref[c] = jnp.broadcast_to(jnp.sqrt(jnp.max(best, axis=0, keepdims=True)),
                                         (SUBLANES, LANES))

    def block(row0, n_rows, k0, n_keys_blk, triangle):
        rows = slice(row0, row0 + n_rows)
        keys = pl.ds(k0, n_keys_blk)
        v = v_ref[keys, :]
        for c in range(2):
            s = lax.dot_general(q_ref[rows, map_lanes(c)], k_ref[keys, map_lanes(c)],
                                (((1,), (1,)), ((), ())), preferred_element_type=F32)
            if triangle:
                r_idx = lax.broadcasted_iota(jnp.int32, s.shape, 0)
                c_idx = lax.broadcasted_iota(jnp.int32, s.shape, 1)
                s = jnp.where(r_idx >= c_idx, s, MASK_VALUE)
            p = jnp.exp2(s - pltpu.repeat(b_ref[c, rows, :], n_keys_blk // LANES, axis=1))
            psum = p[:, 0:LANES]
            for t in range(1, n_keys_blk // LANES):
                psum = psum + p[:, t * LANES:(t + 1) * LANES]
            l_ref[c, rows, :] += psum
            acc_ref[c, rows, :] += _dot(p.astype(BF16), v)

    def tile(k0, diagonal):
        if not diagonal:
            block(0, tq, k0, tk, False)
        else:
            half = tk // 2
            block(0, tq, k0, half, True)
            block(half, tq - half, pl.multiple_of(k0 + half, half), half, True)

    def start(t):
        return pl.multiple_of(t * tk, tk)

    def sweep():
        l_ref[...] = jnp.zeros(l_ref.shape, F32)
        acc_ref[...] = jnp.zeros(acc_ref.shape, F32)

        def trip(g, carry):
            for u in range(TILES_PER_TRIP):
                tile(start(TILES_PER_TRIP * g + u), False)
            return carry

        lax.fori_loop(0, qi // TILES_PER_TRIP, trip, 0)
        base = (qi // TILES_PER_TRIP) * TILES_PER_TRIP
        rem = qi % TILES_PER_TRIP

        @pl.when(rem >= 2)
        def _():
            tile(start(base), False)
            tile(start(base + 1), False)

        last = base + jnp.where(rem >= 2, 2, 0)

        @pl.when(rem % 2 == 1)
        def _():
            tile(start(last), False)
            tile(start(qi), True)

        @pl.when(rem % 2 == 0)
        def _():
            tile(start(qi), True)

    def exact_row_max():
        for c in range(2):
            def fold(best, s):
                for u in range(n_lane_tiles):
                    best = jnp.maximum(best, s[:, u * LANES:(u + 1) * LANES])
                return best

            best = lax.fori_loop(0, qi, lambda t, best: fold(best, scores(start(t), c, False)),
                                 jnp.full((tq, LANES), MASK_VALUE, F32))
            best = fold(best, scores(start(qi), c, True))
            b_ref[c] = jnp.broadcast_to(jnp.max(best, axis=1, keepdims=True), (tq, LANES))

    for c in range(2):
        qq = q_ref[:, map_lanes(c)].astype(F32)
        b_ref[c] = jnp.sqrt(jnp.sum(qq * qq, axis=1, keepdims=True)) * kn_ref[c, 0:1, :]

    def one_pass(state):
        attempt, _ = state

        @pl.when(attempt == 1)
        def _():
            exact_row_max()

        sweep()
        l_min = jnp.min(jnp.minimum(jnp.sum(l_ref[0], axis=1, keepdims=True),
                                    jnp.sum(l_ref[1], axis=1, keepdims=True)))
        done = jnp.logical_or(l_min >= L_FLOOR, attempt == 1)
        return attempt + 1, done.astype(jnp.int32)

    lax.while_loop(lambda state: state[1] == 0, one_pass, (jnp.int32(0), jnp.int32(0)))

    lam = (jnp.exp(jnp.sum(lq1_ref[...] * lk1_ref[...], keepdims=True))
           - jnp.exp(jnp.sum(lq2_ref[...] * lk2_ref[...], keepdims=True)) + lambda_init)
    l0 = jnp.sum(l_ref[0], axis=1, keepdims=True)
    l1 = jnp.sum(l_ref[1], axis=1, keepdims=True)
    o = acc_ref[0] / l0 - lam * (acc_ref[1] / l1)
    o_ref[...] = (_rms_rows(o, sg_ref[...]) * (1.0 - lambda_init)).astype(o_ref.dtype)


def _diff_attention(qkv, lq1, lk1, lq2, lk2, subln_g, lambda_init, casts, tq):
    s = qkv.shape[0]
    vd = 2 * HEAD_DIM
    heads = qkv.shape[1] // (3 * vd)
    vec = pl.BlockSpec((1, HEAD_DIM), lambda h, i: (0, 0))
    return _host_call(
        functools.partial(_diff_attn_kernel, lambda_init=lambda_init), casts,
        grid=(heads, s // tq),
        in_specs=[
            pl.BlockSpec((tq, vd), lambda h, i: (i, h)),
            pl.BlockSpec((s, vd), lambda h, i: (0, heads + h)),
            pl.BlockSpec((s, vd), lambda h, i: (0, 2 * heads + h)),
            vec, vec, vec, vec,
            pl.BlockSpec((1, vd), lambda h, i: (0, 0)),
        ],
        out_spec=pl.BlockSpec((tq, vd), lambda h, i: (i, h)),
        out_shape=jax.ShapeDtypeStruct((s, heads * vd), BF16),
        scratch_shapes=[pltpu.VMEM((2, tq, LANES), F32), pltpu.VMEM((2, tq, LANES), F32),
                        pltpu.VMEM((2, tq, vd), F32), pltpu.VMEM((2, SUBLANES, LANES), F32)],
        name="diff_attention",
        args=(qkv, qkv, qkv, lq1, lk1, lq2, lk2, subln_g))


def _proj_residual_kernel(a_ref, h_ref, w_ref, pg_ref, o_ref):
    o_ref[...] = h_ref[...] + _rms_rows(_dot(a_ref[...], w_ref[...]), pg_ref[...])


def _proj_residual(a, h, w, post_g, tm):
    s, d = h.shape
    row = lambda i: (i, 0)
    return pl.pallas_call(
        _proj_residual_kernel,
        grid=(s // tm,),
        in_specs=[pl.BlockSpec((tm, a.shape[1]), row), pl.BlockSpec((tm, d), row),
                  _resident(w.shape), _resident((1, d))],
        out_specs=pl.BlockSpec((tm, d), row),
        out_shape=jax.ShapeDtypeStruct((s, d), F32),
        compiler_params=_params(1),
        name="proj_residual",
    )(a, h, w, post_g)


def _tiles(s, d, f):
    ffn_cols = 512 if f % 512 == 0 else 256
    return dict(
        proj_rows=min(1024, s), proj_cols=min(512, d), qkv_cols=min(1024, d),
        conv_rows=min(256, s),
        ffn_rows=min(1024, s), ffn_cols=min(ffn_cols, f),
        ple_rows=min(512, s),
        attn_rows=min(1024, s),
        out_rows=min(512, s),
    )


def kernel(x, p, pre_mix_g, post_mix_g, conv_w_pw1, conv_b_pw1, conv_w_dw, conv_b_dw, conv_ln_g, conv_ln_b, conv_w_pw2, conv_b_pw2, attn_w_qkv, attn_lambda_q1, attn_lambda_k1, attn_lambda_q2, attn_lambda_k2, attn_subln_g, attn_w_o, pre_ffn_g, post_ffn_g, ffn_w_in, ffn_w_dw, ffn_w_out, ple_pre_g, ple_w_gate, ple_b_gate, ple_w_proj, ple_post_g):
    batch, s, d = x.shape
    depth = p.shape[0]
    f = ffn_w_out.shape[1]
    t = _tiles(s, d, f)
    vec = lambda a: a.reshape(1, -1)
    p_all = p.reshape(depth * batch, s, p.shape[-1])
    first_proj = lambda i: (conv_w_pw1, i // 2) if i % 2 == 0 else (attn_w_qkv, i // 2)
    outs = []
    for b in range(batch):
        h = x.reshape(s, d) if batch == 1 else x[b]
        w_first = first_proj(0)[0][0]
        for i in range(depth):
            j = i // 2
            layer_casts = [(ffn_w_in, i), (ffn_w_out, i), (ple_w_gate, i), (ple_w_proj, i)]
            if i % 2 == 0:
                u, w_pw2_bf = _norm_glu(h, vec(pre_mix_g[i]), w_first, vec(conv_b_pw1[j]),
                                        [(conv_w_pw2, j)], t["proj_rows"], t["proj_cols"])
                h, w_in_bf, w_out_bf, w_gate_bf, w_proj_bf = _conv_mix(
                    u, h, conv_w_dw[j], vec(conv_b_dw[j]), vec(conv_ln_g[j]), vec(conv_ln_b[j]),
                    w_pw2_bf, vec(conv_b_pw2[j]), vec(post_mix_g[i]), layer_casts, t["conv_rows"])
            else:
                q_scale = HEAD_DIM ** -0.5 * math.log2(math.e)
                col_scale = jnp.concatenate([jnp.full((1, d), q_scale, F32),
                                             jnp.ones((1, 2 * d), F32)], axis=1)
                qkv, w_o_bf = _norm_matmul(h, vec(pre_mix_g[i]), w_first, col_scale,
                                           [(attn_w_o, j)], t["proj_rows"], t["qkv_cols"], BF16)
                lambda_init = 0.8 - 0.6 * math.exp(-0.3 * i)
                o, w_in_bf, w_out_bf, w_gate_bf, w_proj_bf = _diff_attention(
                    qkv, vec(attn_lambda_q1[j]), vec(attn_lambda_k1[j]), vec(attn_lambda_q2[j]),
                    vec(attn_lambda_k2[j]), vec(attn_subln_g[j]), lambda_init,
                    layer_casts, t["attn_rows"])
                h = _proj_residual(o, h, w_o_bf, vec(post_mix_g[i]), t["out_rows"])
            h = _conv_ffn(h, vec(pre_ffn_g[i]), w_in_bf, ffn_w_dw, w_out_bf, i,
                          vec(post_ffn_g[i]), t["ffn_rows"], t["ffn_cols"])
            next_casts = [first_proj(i + 1)] if i + 1 < depth else []
            h, *w_next = _per_layer_embed(h, p_all, i * batch + b, vec(ple_pre_g[i]), w_gate_bf,
                                          vec(ple_b_gate[i]), w_proj_bf, vec(ple_post_g[i]),
                                          next_casts, t["ple_rows"])
            w_first = w_next[0] if w_next else None
        outs.append(h)
    return outs[0].reshape(1, s, d) if batch == 1 else jnp.stack(outs)
```

```python
import functools
import math

import jax
import jax.numpy as jnp
from jax import lax
from jax.experimental import pallas as pl
from jax.experimental.pallas import tpu as pltpu

F32 = jnp.float32
BF16 = jnp.bfloat16

RMS_EPS = 1e-6
LN_EPS = 1e-5
MASK_VALUE = -1e30
HEAD_DIM = 128
CONV_HALO = 32
SUBLANES = 8
LANES = 128
V7X_VMEM_LIMIT_BYTES = 56 * 1024 * 1024


def _params(n_axes, vmem_bytes=V7X_VMEM_LIMIT_BYTES):
    return pltpu.CompilerParams(dimension_semantics=("arbitrary",) * n_axes,
                                vmem_limit_bytes=vmem_bytes)


def _resident(shape):
    return pl.BlockSpec(shape, lambda *_: (0,) * len(shape), pipeline_mode=pl.Buffered(1))


def _cast_sidecar(w, layer, n_steps, step_of):
    rows, cols = w.shape[1:]
    bf16_rows = 2 * SUBLANES
    n_blocks = max(nb for nb in range(1, n_steps + 1)
                   if n_steps % nb == 0 and rows % nb == 0 and (rows // nb) % bf16_rows == 0)
    per, blk = n_steps // n_blocks, rows // n_blocks
    in_spec = pl.BlockSpec((None, blk, cols), lambda *g: (layer, step_of(*g) // per, 0))
    out_spec = pl.BlockSpec((blk, cols), lambda *g: (step_of(*g) // per, 0))
    return in_spec, out_spec, jax.ShapeDtypeStruct((rows, cols), BF16), per


def _host_call(body, casts, grid, in_specs, out_spec, out_shape, scratch_shapes, name, args):
    n_steps = math.prod(grid)

    def step_of(*g):
        step = 0
        for idx, extent in zip(g, grid):
            step = step * extent + idx
        return step

    sidecars = [_cast_sidecar(w, layer, n_steps, step_of) for w, layer in casts]
    n_in, n_cast = len(in_specs), len(casts)

    def kernel(*refs):
        ins, cast_ins = refs[:n_in], refs[n_in:n_in + n_cast]
        out, cast_outs = refs[n_in + n_cast], refs[n_in + n_cast + 1:n_in + 2 * n_cast + 1]
        scratch = refs[n_in + 2 * n_cast + 1:]
        step = step_of(*(pl.program_id(a) for a in range(len(grid))))
        for (_, _, _, per), src, dst in zip(sidecars, cast_ins, cast_outs):
            @pl.when(step % per == 0)
            def _():
                dst[...] = src[...].astype(dst.dtype)
        body(*ins, out, *scratch)

    return pl.pallas_call(
        kernel,
        grid=grid,
        in_specs=list(in_specs) + [sc[0] for sc in sidecars],
        out_specs=(out_spec,) + tuple(sc[1] for sc in sidecars),
        out_shape=(out_shape,) + tuple(sc[2] for sc in sidecars),
        scratch_shapes=scratch_shapes,
        compiler_params=_params(len(grid)),
        name=name,
    )(*args, *(w for w, _ in casts))


def _rms_rows(x, g):
    ms = jnp.mean(x * x, axis=-1, keepdims=True)
    return x * lax.rsqrt(ms + RMS_EPS) * g


def _sigmoid(x):
    return 1.0 / (1.0 + jnp.exp(-x))


def _dot(a, b):
    return jnp.dot(a, b, preferred_element_type=F32)


def _as_bf16(w):
    return w if w.dtype == BF16 else w.astype(BF16)


def _norm_to_scratch(dst_ref, src_ref, g, rows, dst_off=0, chunk=256):
    chunk = min(chunk, rows)

    def body(c, carry):
        r = pl.multiple_of(c * chunk, chunk)
        x = src_ref[pl.ds(r, chunk), :]
        dst_ref[pl.ds(dst_off + r, chunk), :] = _rms_rows(x, g).astype(dst_ref.dtype)
        return carry

    lax.fori_loop(0, rows // chunk, body, 0)


def _norm_glu_kernel(x_ref, g_ref, wa_ref, wg_ref, ba_ref, bg_ref, o_ref, xn_ref):
    @pl.when(pl.program_id(1) == 0)
    def _():
        _norm_to_scratch(xn_ref, x_ref, g_ref[...], x_ref.shape[0])

    xn = xn_ref[...]
    a = _dot(xn, _as_bf16(wa_ref[...])) + ba_ref[...]
    gate = _dot(xn, _as_bf16(wg_ref[...])) + bg_ref[...]
    o_ref[...] = (a * _sigmoid(gate)).astype(o_ref.dtype)


def _norm_glu(x, g, w, b, casts, tm, tn):
    s, d = x.shape
    n = w.shape[1] // 2
    nj = n // tn
    return _host_call(
        _norm_glu_kernel, casts,
        grid=(s // tm, nj),
        in_specs=[
            pl.BlockSpec((tm, d), lambda i, j: (i, 0)),
            pl.BlockSpec((1, d), lambda i, j: (0, 0)),
            pl.BlockSpec((d, tn), lambda i, j: (0, j)),
            pl.BlockSpec((d, tn), lambda i, j: (0, j + nj)),
            pl.BlockSpec((1, tn), lambda i, j: (0, j)),
            pl.BlockSpec((1, tn), lambda i, j: (0, j + nj)),
        ],
        out_spec=pl.BlockSpec((tm, tn), lambda i, j: (i, j)),
        out_shape=jax.ShapeDtypeStruct((s, n), F32),
        scratch_shapes=[pltpu.VMEM((tm, d), BF16)],
        name="norm_glu",
        args=(x, g, w, w, b, b))


def _norm_matmul_kernel(x_ref, g_ref, w_ref, cs_ref, o_ref, xn_ref):
    @pl.when(pl.program_id(1) == 0)
    def _():
        _norm_to_scratch(xn_ref, x_ref, g_ref[...], x_ref.shape[0])

    o_ref[...] = (_dot(xn_ref[...], _as_bf16(w_ref[...])) * cs_ref[...]).astype(o_ref.dtype)


def _norm_matmul(x, g, w, col_scale, casts, tm, tn, out_dtype):
    s, d = x.shape
    n = w.shape[1]
    return _host_call(
        _norm_matmul_kernel, casts,
        grid=(s // tm, n // tn),
        in_specs=[
            pl.BlockSpec((tm, d), lambda i, j: (i, 0)),
            pl.BlockSpec((1, d), lambda i, j: (0, 0)),
            pl.BlockSpec((d, tn), lambda i, j: (0, j)),
            pl.BlockSpec((1, tn), lambda i, j: (0, j)),
        ],
        out_spec=pl.BlockSpec((tm, tn), lambda i, j: (i, j)),
        out_shape=jax.ShapeDtypeStruct((s, n), out_dtype),
        scratch_shapes=[pltpu.VMEM((tm, d), BF16)],
        name="norm_matmul",
        args=(x, g, w, col_scale))


def _dwconv_window(win, w8_ref, lanes, rows, k):
    first = CONV_HALO - (k - 1)
    n_win = rows + CONV_HALO
    acc = None
    for b in range(SUBLANES):
        offs = [o for o in range(b, CONV_HALO + 1, SUBLANES) if 0 <= o - first < k]
        if not offs:
            continue
        shifted = win if b == 0 else pltpu.roll(win, n_win - b, axis=0)
        for o in offs:
            tap = pltpu.repeat(w8_ref[o - first, :, lanes], rows // SUBLANES, axis=0)
            term = shifted[o - b:o - b + rows] * tap
            acc = term if acc is None else acc + term
    return acc


def _conv_mix_kernel(u_ref, halo_ref, h_ref, wdw_ref, bdw_ref, lng_ref, lnb_ref, w2_ref, b2_ref,
                     pg_ref, o_ref, ext_ref, act_ref, w8_ref, *, k, rows):
    tm, d = u_ref.shape
    ext_ref[0:CONV_HALO, :] = jnp.where(pl.program_id(0) > 0, halo_ref[...], 0.0)
    ext_ref[CONV_HALO:, :] = u_ref[...]
    for j in range(k):
        w8_ref[j] = jnp.broadcast_to(wdw_ref[j:j + 1, :], (SUBLANES, d))

    n_lane = d // LANES

    def conv_body(c, carry):
        r0 = pl.multiple_of((c // n_lane) * rows, rows)
        lanes = pl.ds(pl.multiple_of((c % n_lane) * LANES, LANES), LANES)
        win = ext_ref[pl.ds(r0, rows + CONV_HALO), lanes]
        y = _dwconv_window(win, w8_ref, lanes, rows, k) + bdw_ref[:, lanes]
        ext_ref[pl.ds(r0, rows), lanes] = y
        return carry

    lax.fori_loop(0, (tm // rows) * n_lane, conv_body, 0)

    chunk = min(256, tm)

    def ln_body(c, carry):
        r = pl.multiple_of(c * chunk, chunk)
        x = ext_ref[pl.ds(r, chunk), :]
        mu = jnp.mean(x, axis=-1, keepdims=True)
        xc = x - mu
        y = xc * lax.rsqrt(jnp.mean(xc * xc, axis=-1, keepdims=True) + LN_EPS)
        y = y * lng_ref[...] + lnb_ref[...]
        act_ref[pl.ds(r, chunk), :] = (y * _sigmoid(y)).astype(act_ref.dtype)
        return carry

    lax.fori_loop(0, tm // chunk, ln_body, 0)

    y = _dot(act_ref[...], w2_ref[...]) + b2_ref[...]
    o_ref[...] = h_ref[...] + _rms_rows(y, pg_ref[...])


def _conv_mix(u, h, w_dw, b_dw, ln_g, ln_b, w2, b2, post_g, casts, tm):
    s, d = u.shape
    k = w_dw.shape[0]
    halo_blocks = tm // CONV_HALO
    row = lambda i: (i, 0)
    return _host_call(
        functools.partial(_conv_mix_kernel, k=k, rows=min(128, tm)), casts,
        grid=(s // tm,),
        in_specs=[
            pl.BlockSpec((tm, d), row),
            pl.BlockSpec((CONV_HALO, d), lambda i: (jnp.maximum(i * halo_blocks - 1, 0), 0)),
            pl.BlockSpec((tm, d), row),
            _resident((k, d)),
            _resident((1, d)),
            _resident((1, d)),
            _resident((1, d)),
            _resident((d, d)),
            _resident((1, d)),
            _resident((1, d)),
        ],
        out_spec=pl.BlockSpec((tm, d), row),
        out_shape=jax.ShapeDtypeStruct((s, d), F32),
        scratch_shapes=[pltpu.VMEM((tm + CONV_HALO, d), F32), pltpu.VMEM((tm, d), BF16),
                        pltpu.VMEM((k, SUBLANES, d), F32)],
        name="conv_mix",
        args=(u, u, h, w_dw, b_dw, ln_g, ln_b, w2, b2, post_g))


def _causal3(u, tail, w):
    tm = u.shape[0]
    ext = jnp.concatenate([tail, u], axis=0)
    return (u * w[2:3]
            + ext[SUBLANES - 1:SUBLANES - 1 + tm] * w[1:2]
            + ext[SUBLANES - 2:SUBLANES - 2 + tm] * w[0:1])


def _ffn_kernel(h_ref, g_ref, wg_ref, wu_ref, dwg_ref, dwu_ref, wo_ref, pg_ref,
                o_ref, xn_ref, tail_ref):
    i = pl.program_id(0)
    j = pl.program_id(1)
    tm = h_ref.shape[0]

    @pl.when(j == 0)
    def _():
        _norm_to_scratch(xn_ref, h_ref, g_ref[...], tm)
        o_ref[...] = jnp.zeros(o_ref.shape, o_ref.dtype)

    @pl.when(i == 0)
    def _():
        tail_ref[j] = jnp.zeros(tail_ref.shape[1:], F32)

    xn = xn_ref[...]
    pre_gate = _dot(xn, wg_ref[...])
    pre_up = _dot(xn, wu_ref[...])
    gate = _causal3(pre_gate, tail_ref[j, 0], dwg_ref[...])
    up = _causal3(pre_up, tail_ref[j, 1], dwu_ref[...])
    tail_ref[j, 0] = pre_gate[tm - SUBLANES:]
    tail_ref[j, 1] = pre_up[tm - SUBLANES:]
    act = (gate * _sigmoid(gate) * up).astype(BF16)
    o_ref[...] += _dot(act, wo_ref[...])

    @pl.when(j == pl.num_programs(1) - 1)
    def _():
        chunk = min(256, tm)

        def body(c, carry):
            r = pl.multiple_of(c * chunk, chunk)
            rows = pl.ds(r, chunk)
            o_ref[rows, :] = h_ref[rows, :] + _rms_rows(o_ref[rows, :], pg_ref[...])
            return carry

        lax.fori_loop(0, tm // chunk, body, 0)


def _conv_ffn(h, pre_g, w_in, w_dw, w_out, layer, post_g, tm, tf):
    s, d = h.shape
    f = w_out.shape[0]
    nj = f // tf
    return pl.pallas_call(
        _ffn_kernel,
        grid=(s // tm, nj),
        in_specs=[
            pl.BlockSpec((tm, d), lambda i, j: (i, 0), pipeline_mode=pl.Buffered(1)),
            pl.BlockSpec((1, d), lambda i, j: (0, 0)),
            pl.BlockSpec((d, tf), lambda i, j: (0, j)),
            pl.BlockSpec((d, tf), lambda i, j: (0, j + nj)),
            pl.BlockSpec((None, 3, tf), lambda i, j: (layer, 0, j)),
            pl.BlockSpec((None, 3, tf), lambda i, j: (layer, 0, j + nj)),
            pl.BlockSpec((tf, d), lambda i, j: (j, 0)),
            pl.BlockSpec((1, d), lambda i, j: (0, 0)),
        ],
        out_specs=pl.BlockSpec((tm, d), lambda i, j: (i, 0)),
        out_shape=jax.ShapeDtypeStruct((s, d), F32),
        scratch_shapes=[pltpu.VMEM((tm, d), BF16), pltpu.VMEM((nj, 2, SUBLANES, tf), F32)],
        compiler_params=_params(2),
        name="conv_ffn",
    )(h, pre_g, w_in, w_in, w_dw, w_dw, w_out, post_g)


def _ple_kernel(h_ref, p_ref, g_ref, wg_ref, bg_ref, wp_ref, pg_ref, o_ref):
    h = h_ref[...]
    xn = _rms_rows(h, g_ref[...]).astype(BF16)
    gate = _sigmoid(_dot(xn, wg_ref[...]) + bg_ref[...])
    emb = _dot(p_ref[...].astype(BF16), wp_ref[...])
    o_ref[...] = h + _rms_rows(gate * emb, pg_ref[...])


def _per_layer_embed(h, p, p_index, pre_g, w_gate, b_gate, w_proj, post_g, casts, tm):
    s, d = h.shape
    e = p.shape[2]
    row = lambda i: (i, 0)
    return _host_call(
        _ple_kernel, casts,
        grid=(s // tm,),
        in_specs=[
            pl.BlockSpec((tm, d), row),
            pl.BlockSpec((None, tm, e), lambda i: (p_index, i, 0)),
            _resident((1, d)),
            _resident((d, d)),
            _resident((1, d)),
            _resident((e, d)),
            _resident((1, d)),
        ],
        out_spec=pl.BlockSpec((tm, d), row),
        out_shape=jax.ShapeDtypeStruct((s, d), F32),
        scratch_shapes=[],
        name="per_layer_embed",
        args=(h, p, pre_g, w_gate, b_gate, w_proj, post_g))


L_FLOOR = 2.0 ** -64
TILES_PER_TRIP = 2


def _diff_attn_kernel(q_ref, k_ref, v_ref, lq1_ref, lk1_ref, lq2_ref, lk2_ref, sg_ref, o_ref,
                      b_ref, l_ref, acc_ref, kn_ref, *, lambda_init):
    qi = pl.program_id(1)
    tq = q_ref.shape[0]
    tk = tq
    n_lane_tiles = tk // LANES
    n_keys = k_ref.shape[0]

    def map_lanes(c):
        return slice(c * HEAD_DIM, (c + 1) * HEAD_DIM)

    def scores(k0, c, diagonal):
        s = lax.dot_general(q_ref[:, map_lanes(c)], k_ref[pl.ds(k0, tk), map_lanes(c)],
                            (((1,), (1,)), ((), ())), preferred_element_type=F32)
        if diagonal:
            rows = lax.broadcasted_iota(jnp.int32, s.shape, 0)
            cols = lax.broadcasted_iota(jnp.int32, s.shape, 1)
            s = jnp.where(rows >= cols, s, MASK_VALUE)
        return s

    @pl.when(qi == 0)
    def _():
        chunk = min(512, n_keys)
        for c in range(2):
            def body(i, best):
                rows = pl.ds(pl.multiple_of(i * chunk, chunk), chunk)
                kk = k_ref[rows, map_lanes(c)].astype(F32)
                return jnp.maximum(best, jnp.sum(kk * kk, axis=1, keepdims=True))

            best = lax.fori_loop(0, n_keys // chunk, body, jnp.zeros((chunk, 1), F32))
            kn_ref[c] = jnp.broadcast_to(jnp.sqrt(jnp.max(best, axis=0, keepdims=True)),
                                         (SUBLANES, LANES))

    def block(row0, n_rows, k0, n_keys_blk, triangle):
        rows = slice(row0, row0 + n_rows)
        keys = pl.ds(k0, n_keys_blk)
        v = v_ref[keys, :]
        for c in range(2):
            s = lax.dot_general(q_ref[rows, map_lanes(c)], k_ref[keys, map_lanes(c)],
                                (((1,), (1,)), ((), ())), preferred_element_type=F32)
            if triangle:
                r_idx = lax.broadcasted_iota(jnp.int32, s.shape, 0)
                c_idx = lax.broadcasted_iota(jnp.int32, s.shape, 1)
                s = jnp.where(r_idx >= c_idx, s, MASK_VALUE)
            p = jnp.exp2(s - pltpu.repeat(b_ref[c, rows, :], n_keys_blk // LANES, axis=1))
            psum = p[:, 0:LANES]
            for t in range(1, n_keys_blk // LANES):
                psum = psum + p[:, t * LANES:(t + 1) * LANES]
            l_ref[c, rows, :] += psum
            acc_ref[c, rows, :] += _dot(p.astype(BF16), v)

    def tile(k0, diagonal):
        if not diagonal:
            block(0, tq, k0, tk, False)
        else:
            half = tk // 2
            block(0, tq, k0, half, True)
            block(half, tq - half, pl.multiple_of(k0 + half, half), half, True)

    def start(t):
        return pl.multiple_of(t * tk, tk)

    def sweep():
        l_ref[...] = jnp.zeros(l_ref.shape, F32)
        acc_ref[...] = jnp.zeros(acc_ref.shape, F32)

        def trip(g, carry):
            for u in range(TILES_PER_TRIP):
                tile(start(TILES_PER_TRIP * g + u), False)
            return carry

        lax.fori_loop(0, qi // TILES_PER_TRIP, trip, 0)
        base = (qi // TILES_PER_TRIP) * TILES_PER_TRIP
        rem = qi % TILES_PER_TRIP

        last = base
        if TILES_PER_TRIP == 4:
            @pl.when(rem >= 2)
            def _():
                tile(start(base), False)
                tile(start(base + 1), False)

            last = base + jnp.where(rem >= 2, 2, 0)

        @pl.when(rem % 2 == 1)
        def _():
            tile(start(last), False)
            tile(start(qi), True)

        @pl.when(rem % 2 == 0)
        def _():
            tile(start(qi), True)

    def exact_row_max():
        for c in range(2):
            def fold(best, s):
                for u in range(n_lane_tiles):
                    best = jnp.maximum(best, s[:, u * LANES:(u + 1) * LANES])
                return best

            best = lax.fori_loop(0, qi, lambda t, best: fold(best, scores(start(t), c, False)),
                                 jnp.full((tq, LANES), MASK_VALUE, F32))
            best = fold(best, scores(start(qi), c, True))
            b_ref[c] = jnp.broadcast_to(jnp.max(best, axis=1, keepdims=True), (tq, LANES))

    for c in range(2):
        qq = q_ref[:, map_lanes(c)].astype(F32)
        b_ref[c] = jnp.sqrt(jnp.sum(qq * qq, axis=1, keepdims=True)) * kn_ref[c, 0:1, :]

    def one_pass(state):
        attempt, _ = state

        @pl.when(attempt == 1)
        def _():
            exact_row_max()

        sweep()
        l_min = jnp.min(jnp.minimum(jnp.sum(l_ref[0], axis=1, keepdims=True),
                                    jnp.sum(l_ref[1], axis=1, keepdims=True)))
        done = jnp.logical_or(l_min >= L_FLOOR, attempt == 1)
        return attempt + 1, done.astype(jnp.int32)

    lax.while_loop(lambda state: state[1] == 0, one_pass, (jnp.int32(0), jnp.int32(0)))

    lam = (jnp.exp(jnp.sum(lq1_ref[...] * lk1_ref[...], keepdims=True))
           - jnp.exp(jnp.sum(lq2_ref[...] * lk2_ref[...], keepdims=True)) + lambda_init)
    l0 = jnp.sum(l_ref[0], axis=1, keepdims=True)
    l1 = jnp.sum(l_ref[1], axis=1, keepdims=True)
    o = acc_ref[0] / l0 - lam * (acc_ref[1] / l1)
    o_ref[...] = (_rms_rows(o, sg_ref[...]) * (1.0 - lambda_init)).astype(o_ref.dtype)


def _diff_attention(qkv, lq1, lk1, lq2, lk2, subln_g, lambda_init, casts, tq):
    s = qkv.shape[0]
    vd = 2 * HEAD_DIM
    heads = qkv.shape[1] // (3 * vd)
    vec = pl.BlockSpec((1, HEAD_DIM), lambda h, i: (0, 0))
    return _host_call(
        functools.partial(_diff_attn_kernel, lambda_init=lambda_init), casts,
        grid=(heads, s // tq),
        in_specs=[
            pl.BlockSpec((tq, vd), lambda h, i: (i, h)),
            pl.BlockSpec((s, vd), lambda h, i: (0, heads + h)),
            pl.BlockSpec((s, vd), lambda h, i: (0, 2 * heads + h)),
            vec, vec, vec, vec,
            pl.BlockSpec((1, vd), lambda h, i: (0, 0)),
        ],
        out_spec=pl.BlockSpec((tq, vd), lambda h, i: (i, h)),
        out_shape=jax.ShapeDtypeStruct((s, heads * vd), BF16),
        scratch_shapes=[pltpu.VMEM((2, tq, LANES), F32), pltpu.VMEM((2, tq, LANES), F32),
                        pltpu.VMEM((2, tq, vd), F32), pltpu.VMEM((2, SUBLANES, LANES), F32)],
        name="diff_attention",
        args=(qkv, qkv, qkv, lq1, lk1, lq2, lk2, subln_g))


def _proj_residual_kernel(a_ref, h_ref, w_ref, pg_ref, o_ref):
    o_ref[...] = h_ref[...] + _rms_rows(_dot(a_ref[...], w_ref[...]), pg_ref[...])


def _proj_residual(a, h, w, post_g, tm):
    s, d = h.shape
    row = lambda i: (i, 0)
    return pl.pallas_call(
        _proj_residual_kernel,
        grid=(s // tm,),
        in_specs=[pl.BlockSpec((tm, a.shape[1]), row), pl.BlockSpec((tm, d), row),
                  _resident(w.shape), _resident((1, d))],
        out_specs=pl.BlockSpec((tm, d), row),
        out_shape=jax.ShapeDtypeStruct((s, d), F32),
        compiler_params=_params(1),
        name="proj_residual",
    )(a, h, w, post_g)


def _tiles(s, d, f):
    ffn_cols = 512 if f % 512 == 0 else 256
    return dict(
        proj_rows=min(1024, s), proj_cols=min(512, d), qkv_cols=min(1024, d),
        conv_rows=min(256, s),
        ffn_rows=min(1024, s), ffn_cols=min(ffn_cols, f),
        ple_rows=min(512, s),
        attn_rows=min(1024, s),
        out_rows=min(512, s),
    )


def kernel(x, p, pre_mix_g, post_mix_g, conv_w_pw1, conv_b_pw1, conv_w_dw, conv_b_dw, conv_ln_g, conv_ln_b, conv_w_pw2, conv_b_pw2, attn_w_qkv, attn_lambda_q1, attn_lambda_k1, attn_lambda_q2, attn_lambda_k2, attn_subln_g, attn_w_o, pre_ffn_g, post_ffn_g, ffn_w_in, ffn_w_dw, ffn_w_out, ple_pre_g, ple_w_gate, ple_b_gate, ple_w_proj, ple_post_g):
    batch, s, d = x.shape
    depth = p.shape[0]
    f = ffn_w_out.shape[1]
    t = _tiles(s, d, f)
    vec = lambda a: a.reshape(1, -1)
    p_all = p.reshape(depth * batch, s, p.shape[-1])
    first_proj = lambda i: (conv_w_pw1, i // 2) if i % 2 == 0 else (attn_w_qkv, i // 2)
    outs = []
    for b in range(batch):
        h = x.reshape(s, d) if batch == 1 else x[b]
        w_first = first_proj(0)[0][0]
        for i in range(depth):
            j = i // 2
            layer_casts = [(ffn_w_in, i), (ffn_w_out, i), (ple_w_gate, i), (ple_w_proj, i)]
            if i % 2 == 0:
                u, w_pw2_bf = _norm_glu(h, vec(pre_mix_g[i]), w_first, vec(conv_b_pw1[j]),
                                        [(conv_w_pw2, j)], t["proj_rows"], t["proj_cols"])
                h, w_in_bf, w_out_bf, w_gate_bf, w_proj_bf = _conv_mix(
                    u, h, conv_w_dw[j], vec(conv_b_dw[j]), vec(conv_ln_g[j]), vec(conv_ln_b[j]),
                    w_pw2_bf, vec(conv_b_pw2[j]), vec(post_mix_g[i]), layer_casts, t["conv_rows"])
            else:
                q_scale = HEAD_DIM ** -0.5 * math.log2(math.e)
                col_scale = jnp.concatenate([jnp.full((1, d), q_scale, F32),
                                             jnp.ones((1, 2 * d), F32)], axis=1)
                qkv, w_o_bf = _norm_matmul(h, vec(pre_mix_g[i]), w_first, col_scale,
                                           [(attn_w_o, j)], t["proj_rows"], t["qkv_cols"], BF16)
                lambda_init = 0.8 - 0.6 * math.exp(-0.3 * i)
                o, w_in_bf, w_out_bf, w_gate_bf, w_proj_bf = _diff_attention(
                    qkv, vec(attn_lambda_q1[j]), vec(attn_lambda_k1[j]), vec(attn_lambda_q2[j]),
                    vec(attn_lambda_k2[j]), vec(attn_subln_g[j]), lambda_init,
                    layer_casts, t["attn_rows"])
                h = _proj_residual(o, h, w_o_bf, vec(post_mix_g[i]), t["out_rows"])
            h = _conv_ffn(h, vec(pre_ffn_g[i]), w_in_bf, ffn_w_dw, w_out_bf, i,
                          vec(post_ffn_g[i]), t["ffn_rows"], t["ffn_cols"])
            next_casts = [first_proj(i + 1)] if i + 1 < depth else []
            h, *w_next = _per_layer_embed(h, p_all, i * batch + b, vec(ple_pre_g[i]), w_gate_bf,
                                          vec(ple_b_gate[i]), w_proj_bf, vec(ple_post_g[i]),
                                          next_casts, t["ple_rows"])
            w_first = w_next[0] if w_next else None
        outs.append(h)
    return outs[0].reshape(1, s, d) if batch == 1 else jnp.stack(outs)
```

```python
import functools
import math

import jax
import jax.numpy as jnp
from jax import lax
from jax.experimental import pallas as pl
from jax.experimental.pallas import tpu as pltpu

F32 = jnp.float32
BF16 = jnp.bfloat16

RMS_EPS = 1e-6
LN_EPS = 1e-5
MASK_VALUE = -1e30
HEAD_DIM = 128
CONV_HALO = 32
SUBLANES = 8
LANES = 128
V7X_VMEM_LIMIT_BYTES = 56 * 1024 * 1024


def _params(n_axes, vmem_bytes=V7X_VMEM_LIMIT_BYTES):
    return pltpu.CompilerParams(dimension_semantics=("arbitrary",) * n_axes,
                                vmem_limit_bytes=vmem_bytes)


def _resident(shape):
    return pl.BlockSpec(shape, lambda *_: (0,) * len(shape), pipeline_mode=pl.Buffered(1))


def _cast_sidecar(w, layer, n_steps, step_of):
    rows, cols = w.shape[1:]
    bf16_rows = 2 * SUBLANES
    n_blocks = max(nb for nb in range(1, n_steps + 1)
                   if n_steps % nb == 0 and rows % nb == 0 and (rows // nb) % bf16_rows == 0)
    per, blk = n_steps // n_blocks, rows // n_blocks
    in_spec = pl.BlockSpec((None, blk, cols), lambda *g: (layer, step_of(*g) // per, 0))
    out_spec = pl.BlockSpec((blk, cols), lambda *g: (step_of(*g) // per, 0))
    return in_spec, out_spec, jax.ShapeDtypeStruct((rows, cols), BF16), per


def _host_call(body, casts, grid, in_specs, out_spec, out_shape, scratch_shapes, name, args):
    n_steps = math.prod(grid)

    def step_of(*g):
        step = 0
        for idx, extent in zip(g, grid):
            step = step * extent + idx
        return step

    sidecars = [_cast_sidecar(w, layer, n_steps, step_of) for w, layer in casts]
    n_in, n_cast = len(in_specs), len(casts)

    def kernel(*refs):
        ins, cast_ins = refs[:n_in], refs[n_in:n_in + n_cast]
        out, cast_outs = refs[n_in + n_cast], refs[n_in + n_cast + 1:n_in + 2 * n_cast + 1]
        scratch = refs[n_in + 2 * n_cast + 1:]
        step = step_of(*(pl.program_id(a) for a in range(len(grid))))
        for (_, _, _, per), src, dst in zip(sidecars, cast_ins, cast_outs):
            @pl.when(step % per == 0)
            def _():
                dst[...] = src[...].astype(dst.dtype)
        body(*ins, out, *scratch)

    return pl.pallas_call(
        kernel,
        grid=grid,
        in_specs=list(in_specs) + [sc[0] for sc in sidecars],
        out_specs=(out_spec,) + tuple(sc[1] for sc in sidecars),
        out_shape=(out_shape,) + tuple(sc[2] for sc in sidecars),
        scratch_shapes=scratch_shapes,
        compiler_params=_params(len(grid)),
        name=name,
    )(*args, *(w for w, _ in casts))


def _rms_rows(x, g):
    ms = jnp.mean(x * x, axis=-1, keepdims=True)
    return x * lax.rsqrt(ms + RMS_EPS) * g


def _sigmoid(x):
    return 1.0 / (1.0 + jnp.exp(-x))


def _dot(a, b):
    return jnp.dot(a, b, preferred_element_type=F32)


def _as_bf16(w):
    return w if w.dtype == BF16 else w.astype(BF16)


def _norm_to_scratch(dst_ref, src_ref, g, rows, dst_off=0, chunk=256):
    chunk = min(chunk, rows)

    def body(c, carry):
        r = pl.multiple_of(c * chunk, chunk)
        x = src_ref[pl.ds(r, chunk), :]
        dst_ref[pl.ds(dst_off + r, chunk), :] = _rms_rows(x, g).astype(dst_ref.dtype)
        return carry

    lax.fori_loop(0, rows // chunk, body, 0)


def _norm_glu_kernel(x_ref, g_ref, wa_ref, wg_ref, ba_ref, bg_ref, o_ref, xn_ref):
    @pl.when(pl.program_id(1) == 0)
    def _():
        _norm_to_scratch(xn_ref, x_ref, g_ref[...], x_ref.shape[0])

    xn = xn_ref[...]
    a = _dot(xn, _as_bf16(wa_ref[...])) + ba_ref[...]
    gate = _dot(xn, _as_bf16(wg_ref[...])) + bg_ref[...]
    o_ref[...] = (a * _sigmoid(gate)).astype(o_ref.dtype)


def _norm_glu(x, g, w, b, casts, tm, tn):
    s, d = x.shape
    n = w.shape[1] // 2
    nj = n // tn
    return _host_call(
        _norm_glu_kernel, casts,
        grid=(s // tm, nj),
        in_specs=[
            pl.BlockSpec((tm, d), lambda i, j: (i, 0)),
            pl.BlockSpec((1, d), lambda i, j: (0, 0)),
            pl.BlockSpec((d, tn), lambda i, j: (0, j)),
            pl.BlockSpec((d, tn), lambda i, j: (0, j + nj)),
            pl.BlockSpec((1, tn), lambda i, j: (0, j)),
            pl.BlockSpec((1, tn), lambda i, j: (0, j + nj)),
        ],
        out_spec=pl.BlockSpec((tm, tn), lambda i, j: (i, j)),
        out_shape=jax.ShapeDtypeStruct((s, n), F32),
        scratch_shapes=[pltpu.VMEM((tm, d), BF16)],
        name="norm_glu",
        args=(x, g, w, w, b, b))


def _norm_matmul_kernel(x_ref, g_ref, w_ref, cs_ref, o_ref, xn_ref):
    @pl.when(pl.program_id(1) == 0)
    def _():
        _norm_to_scratch(xn_ref, x_ref, g_ref[...], x_ref.shape[0])

    o_ref[...] = (_dot(xn_ref[...], _as_bf16(w_ref[...])) * cs_ref[...]).astype(o_ref.dtype)


def _norm_matmul(x, g, w, col_scale, casts, tm, tn, out_dtype):
    s, d = x.shape
    n = w.shape[1]
    return _host_call(
        _norm_matmul_kernel, casts,
        grid=(s // tm, n // tn),
        in_specs=[
            pl.BlockSpec((tm, d), lambda i, j: (i, 0)),
            pl.BlockSpec((1, d), lambda i, j: (0, 0)),
            pl.BlockSpec((d, tn), lambda i, j: (0, j)),
            pl.BlockSpec((1, tn), lambda i, j: (0, j)),
        ],
        out_spec=pl.BlockSpec((tm, tn), lambda i, j: (i, j)),
        out_shape=jax.ShapeDtypeStruct((s, n), out_dtype),
        scratch_shapes=[pltpu.VMEM((tm, d), BF16)],
        name="norm_matmul",
        args=(x, g, w, col_scale))


def _dwconv_window(win, w8_ref, lanes, rows, k):
    first = CONV_HALO - (k - 1)
    n_win = rows + CONV_HALO
    acc = None
    for b in range(SUBLANES):
        offs = [o for o in range(b, CONV_HALO + 1, SUBLANES) if 0 <= o - first < k]
        if not offs:
            continue
        shifted = win if b == 0 else pltpu.roll(win, n_win - b, axis=0)
        for o in offs:
            tap = pltpu.repeat(w8_ref[o - first, :, lanes], rows // SUBLANES, axis=0)
            term = shifted[o - b:o - b + rows] * tap
            acc = term if acc is None else acc + term
    return acc


def _conv_mix_kernel(u_ref, halo_ref, h_ref, wdw_ref, bdw_ref, lng_ref, lnb_ref, w2_ref, b2_ref,
                     pg_ref, o_ref, ext_ref, act_ref, w8_ref, *, k, rows):
    tm, d = u_ref.shape
    ext_ref[0:CONV_HALO, :] = jnp.where(pl.program_id(0) > 0, halo_ref[...], 0.0)
    ext_ref[CONV_HALO:, :] = u_ref[...]
    for j in range(k):
        w8_ref[j] = jnp.broadcast_to(wdw_ref[j:j + 1, :], (SUBLANES, d))

    n_lane = d // LANES

    def conv_body(c, carry):
        r0 = pl.multiple_of((c // n_lane) * rows, rows)
        lanes = pl.ds(pl.multiple_of((c % n_lane) * LANES, LANES), LANES)
        win = ext_ref[pl.ds(r0, rows + CONV_HALO), lanes]
        y = _dwconv_window(win, w8_ref, lanes, rows, k) + bdw_ref[:, lanes]
        ext_ref[pl.ds(r0, rows), lanes] = y
        return carry

    lax.fori_loop(0, (tm // rows) * n_lane, conv_body, 0)

    chunk = min(256, tm)

    def ln_body(c, carry):
        r = pl.multiple_of(c * chunk, chunk)
        x = ext_ref[pl.ds(r, chunk), :]
        mu = jnp.mean(x, axis=-1, keepdims=True)
        xc = x - mu
        y = xc * lax.rsqrt(jnp.mean(xc * xc, axis=-1, keepdims=True) + LN_EPS)
        y = y * lng_ref[...] + lnb_ref[...]
        act_ref[pl.ds(r, chunk), :] = (y * _sigmoid(y)).astype(act_ref.dtype)
        return carry

    lax.fori_loop(0, tm // chunk, ln_body, 0)

    y = _dot(act_ref[...], w2_ref[...]) + b2_ref[...]
    o_ref[...] = h_ref[...] + _rms_rows(y, pg_ref[...])


def _conv_mix(u, h, w_dw, b_dw, ln_g, ln_b, w2, b2, post_g, casts, tm):
    s, d = u.shape
    k = w_dw.shape[0]
    halo_blocks = tm // CONV_HALO
    row = lambda i: (i, 0)
    return _host_call(
        functools.partial(_conv_mix_kernel, k=k, rows=min(128, tm)), casts,
        grid=(s // tm,),
        in_specs=[
            pl.BlockSpec((tm, d), row),
            pl.BlockSpec((CONV_HALO, d), lambda i: (jnp.maximum(i * halo_blocks - 1, 0), 0)),
            pl.BlockSpec((tm, d), row),
            _resident((k, d)),
            _resident((1, d)),
            _resident((1, d)),
            _resident((1, d)),
            _resident((d, d)),
            _resident((1, d)),
            _resident((1, d)),
        ],
        out_spec=pl.BlockSpec((tm, d), row),
        out_shape=jax.ShapeDtypeStruct((s, d), F32),
        scratch_shapes=[pltpu.VMEM((tm + CONV_HALO, d), F32), pltpu.VMEM((tm, d), BF16),
                        pltpu.VMEM((k, SUBLANES, d), F32)],
        name="conv_mix",
        args=(u, u, h, w_dw, b_dw, ln_g, ln_b, w2, b2, post_g))


def _causal3(u, tail, w):
    tm = u.shape[0]
    ext = jnp.concatenate([tail, u], axis=0)
    return (u * w[2:3]
            + ext[SUBLANES - 1:SUBLANES - 1 + tm] * w[1:2]
            + ext[SUBLANES - 2:SUBLANES - 2 + tm] * w[0:1])


def _ffn_kernel(h_ref, g_ref, wg_ref, wu_ref, dwg_ref, dwu_ref, wo_ref, pg_ref,
                o_ref, xn_ref, tail_ref):
    i = pl.program_id(0)
    j = pl.program_id(1)
    tm = h_ref.shape[0]

    @pl.when(j == 0)
    def _():
        _norm_to_scratch(xn_ref, h_ref, g_ref[...], tm)
        o_ref[...] = jnp.zeros(o_ref.shape, o_ref.dtype)

    @pl.when(i == 0)
    def _():
        tail_ref[j] = jnp.zeros(tail_ref.shape[1:], F32)

    xn = xn_ref[...]
    pre_gate = _dot(xn, wg_ref[...])
    pre_up = _dot(xn, wu_ref[...])
    gate = _causal3(pre_gate, tail_ref[j, 0], dwg_ref[...])
    up = _causal3(pre_up, tail_ref[j, 1], dwu_ref[...])
    tail_ref[j, 0] = pre_gate[tm - SUBLANES:]
    tail_ref[j, 1] = pre_up[tm - SUBLANES:]
    act = (gate * _sigmoid(gate) * up).astype(BF16)
    o_ref[...] += _dot(act, wo_ref[...])

    @pl.when(j == pl.num_programs(1) - 1)
    def _():
        chunk = min(256, tm)

        def body(c, carry):
            r = pl.multiple_of(c * chunk, chunk)
            rows = pl.ds(r, chunk)
            o_ref[rows, :] = h_ref[rows, :] + _rms_rows(o_ref[rows, :], pg_ref[...])
            return carry

        lax.fori_loop(0, tm // chunk, body, 0)


def _conv_ffn(h, pre_g, w_in, w_dw, w_out, layer, post_g, tm, tf):
    s, d = h.shape
    f = w_out.shape[0]
    nj = f // tf
    return pl.pallas_call(
        _ffn_kernel,
        grid=(s // tm, nj),
        in_specs=[
            pl.BlockSpec((tm, d), lambda i, j: (i, 0), pipeline_mode=pl.Buffered(1)),
            pl.BlockSpec((1, d), lambda i, j: (0, 0)),
            pl.BlockSpec((d, tf), lambda i, j: (0, j)),
            pl.BlockSpec((d, tf), lambda i, j: (0, j + nj)),
            pl.BlockSpec((None, 3, tf), lambda i, j: (layer, 0, j)),
            pl.BlockSpec((None, 3, tf), lambda i, j: (layer, 0, j + nj)),
            pl.BlockSpec((tf, d), lambda i, j: (j, 0)),
            pl.BlockSpec((1, d), lambda i, j: (0, 0)),
        ],
        out_specs=pl.BlockSpec((tm, d), lambda i, j: (i, 0)),
        out_shape=jax.ShapeDtypeStruct((s, d), F32),
        scratch_shapes=[pltpu.VMEM((tm, d), BF16), pltpu.VMEM((nj, 2, SUBLANES, tf), F32)],
        compiler_params=_params(2),
        name="conv_ffn",
    )(h, pre_g, w_in, w_in, w_dw, w_dw, w_out, post_g)


def _ple_kernel(h_ref, p_ref, g_ref, wg_ref, bg_ref, wp_ref, pg_ref, o_ref):
    h = h_ref[...]
    xn = _rms_rows(h, g_ref[...]).astype(BF16)
    gate = _sigmoid(_dot(xn, wg_ref[...]) + bg_ref[...])
    emb = _dot(p_ref[...].astype(BF16), wp_ref[...])
    o_ref[...] = h + _rms_rows(gate * emb, pg_ref[...])


def _per_layer_embed(h, p, p_index, pre_g, w_gate, b_gate, w_proj, post_g, casts, tm):
    s, d = h.shape
    e = p.shape[2]
    row = lambda i: (i, 0)
    return _host_call(
        _ple_kernel, casts,
        grid=(s // tm,),
        in_specs=[
            pl.BlockSpec((tm, d), row),
            pl.BlockSpec((None, tm, e), lambda i: (p_index, i, 0)),
            _resident((1, d)),
            _resident((d, d)),
            _resident((1, d)),
            _resident((e, d)),
            _resident((1, d)),
        ],
        out_spec=pl.BlockSpec((tm, d), row),
        out_shape=jax.ShapeDtypeStruct((s, d), F32),
        scratch_shapes=[],
        name="per_layer_embed",
        args=(h, p, pre_g, w_gate, b_gate, w_proj, post_g))


L_FLOOR = 2.0 ** -64
TILES_PER_TRIP = 4


def _diff_attn_kernel(q_ref, k_ref, v_ref, lq1_ref, lk1_ref, lq2_ref, lk2_ref, sg_ref, o_ref,
                      b_ref, l_ref, acc_ref, kn_ref, *, lambda_init):
    qi = pl.program_id(1)
    tq = q_ref.shape[0]
    tk = tq
    n_lane_tiles = tk // LANES
    n_keys = k_ref.shape[0]

    def map_lanes(c):
        return slice(c * HEAD_DIM, (c + 1) * HEAD_DIM)

    def scores(k0, c, diagonal):
        s = lax.dot_general(q_ref[:, map_lanes(c)], k_ref[pl.ds(k0, tk), map_lanes(c)],
                            (((1,), (1,)), ((), ())), preferred_element_type=F32)
        if diagonal:
            rows = lax.broadcasted_iota(jnp.int32, s.shape, 0)
            cols = lax.broadcasted_iota(jnp.int32, s.shape, 1)
            s = jnp.where(rows >= cols, s, MASK_VALUE)
        return s

    @pl.when(qi == 0)
    def _():
        chunk = min(512, n_keys)
        for c in range(2):
            def body(i, best):
                rows = pl.ds(pl.multiple_of(i * chunk, chunk), chunk)
                kk = k_ref[rows, map_lanes(c)].astype(F32)
                return jnp.maximum(best, jnp.sum(kk * kk, axis=1, keepdims=True))

            best = lax.fori_loop(0, n_keys // chunk, body, jnp.zeros((chunk, 1), F32))
            kn_ref[c] = jnp.broadcast_to(jnp.sqrt(jnp.max(best, axis=0, keepdims=True)),
                                         (SUBLANES, LANES))

    def block(row0, n_rows, k0, n_keys_blk, triangle):
        rows = slice(row0, row0 + n_rows)
        keys = pl.ds(k0, n_keys_blk)
        v = v_ref[keys, :]
        for c in range(2):
            s = lax.dot_general(q_ref[rows, map_lanes(c)], k_ref[keys, map_lanes(c)],
                                (((1,), (1,)), ((), ())), preferred_element_type=F32)
            if triangle:
                r_idx = lax.broadcasted_iota(jnp.int32, s.shape, 0)
                c_idx = lax.broadcasted_iota(jnp.int32, s.shape, 1)
                s = jnp.where(r_idx >= c_idx, s, MASK_VALUE)
            p = jnp.exp2(s - pltpu.repeat(b_ref[c, rows, :], n_keys_blk // LANES, axis=1))
            psum = p[:, 0:LANES]
            for t in range(1, n_keys_blk // LANES):
                psum = psum + p[:, t * LANES:(t + 1) * LANES]
            l_ref[c, rows, :] += psum
            acc_ref[c, rows, :] += _dot(p.astype(BF16), v)

    def tile(k0, diagonal):
        if not diagonal:
            block(0, tq, k0, tk, False)
        else:
            half = tk // 2
            block(0, tq, k0, half, True)
            block(half, tq - half, pl.multiple_of(k0 + half, half), half, True)

    def start(t):
        return pl.multiple_of(t * tk, tk)

    def sweep():
        l_ref[...] = jnp.zeros(l_ref.shape, F32)
        acc_ref[...] = jnp.zeros(acc_ref.shape, F32)

        def trip(g, carry):
            for u in range(TILES_PER_TRIP):
                tile(start(TILES_PER_TRIP * g + u), False)
            return carry

        lax.fori_loop(0, qi // TILES_PER_TRIP, trip, 0)
        base = (qi // TILES_PER_TRIP) * TILES_PER_TRIP
        rem = qi % TILES_PER_TRIP

        @pl.when(rem >= 2)
        def _():
            tile(start(base), False)
            tile(start(base + 1), False)

        last = base + jnp.where(rem >= 2, 2, 0)

        @pl.when(rem % 2 == 1)
        def _():
            tile(start(last), False)
            tile(start(qi), True)

        @pl.when(rem % 2 == 0)
        def _():
            tile(start(qi), True)

    def exact_row_max():
        for c in range(2):
            def fold(best, s):
                for u in range(n_lane_tiles):
                    best = jnp.maximum(best, s[:, u * LANES:(u + 1) * LANES])
                return best

            best = lax.fori_loop(0, qi, lambda t, best: fold(best, scores(start(t), c, False)),
                                 jnp.full((tq, LANES), MASK_VALUE, F32))
            best = fold(best, scores(start(qi), c, True))
            b_ref[c] = jnp.broadcast_to(jnp.max(best, axis=1, keepdims=True), (tq, LANES))

    for c in range(2):
        qq = q_ref[:, map_lanes(c)].astype(F32)
        b_ref[c] = jnp.sqrt(jnp.sum(qq * qq, axis=1, keepdims=True)) * kn_ref[c, 0:1, :]

    def one_pass(state):
        attempt, _ = state

        @pl.when(attempt == 1)
        def _():
            exact_row_max()

        sweep()
        l_min = jnp.min(jnp.minimum(jnp.sum(l_ref[0], axis=1, keepdims=True),
                                    jnp.sum(l_ref[1], axis=1, keepdims=True)))
        done = jnp.logical_or(l_min >= L_FLOOR, attempt == 1)
        return attempt + 1, done.astype(jnp.int32)

    lax.while_loop(lambda state: state[1] == 0, one_pass, (jnp.int32(0), jnp.int32(0)))

    lam = (jnp.exp(jnp.sum(lq1_ref[...] * lk1_ref[...], keepdims=True))
           - jnp.exp(jnp.sum(lq2_ref[...] * lk2_ref[...], keepdims=True)) + lambda_init)
    l0 = jnp.sum(l_ref[0], axis=1, keepdims=True)
    l1 = jnp.sum(l_ref[1], axis=1, keepdims=True)
    o = acc_ref[0] / l0 - lam * (acc_ref[1] / l1)
    o_ref[...] = (_rms_rows(o, sg_ref[...]) * (1.0 - lambda_init)).astype(o_ref.dtype)


def _diff_attention(qkv, lq1, lk1, lq2, lk2, subln_g, lambda_init, casts, tq):
    s = qkv.shape[0]
    vd = 2 * HEAD_DIM
    heads = qkv.shape[1] // (3 * vd)
    vec = pl.BlockSpec((1, HEAD_DIM), lambda h, i: (0, 0))
    return _host_call(
        functools.partial(_diff_attn_kernel, lambda_init=lambda_init), casts,
        grid=(heads, s // tq),
        in_specs=[
            pl.BlockSpec((tq, vd), lambda h, i: (i, h)),
            pl.BlockSpec((s, vd), lambda h, i: (0, heads + h)),
            pl.BlockSpec((s, vd), lambda h, i: (0, 2 * heads + h)),
            vec, vec, vec, vec,
            pl.BlockSpec((1, vd), lambda h, i: (0, 0)),
        ],
        out_spec=pl.BlockSpec((tq, vd), lambda h, i: (i, h)),
        out_shape=jax.ShapeDtypeStruct((s, heads * vd), BF16),
        scratch_shapes=[pltpu.VMEM((2, tq, LANES), F32), pltpu.VMEM((2, tq, LANES), F32),
                        pltpu.VMEM((2, tq, vd), F32), pltpu.VMEM((2, SUBLANES, LANES), F32)],
        name="diff_attention",
        args=(qkv, qkv, qkv, lq1, lk1, lq2, lk2, subln_g))


def _proj_residual_kernel(a_ref, h_ref, w_ref, pg_ref, o_ref):
    o_ref[...] = h_ref[...] + _rms_rows(_dot(a_ref[...], w_ref[...]), pg_ref[...])


def _proj_residual(a, h, w, post_g, tm):
    s, d = h.shape
    row = lambda i: (i, 0)
    return pl.pallas_call(
        _proj_residual_kernel,
        grid=(s // tm,),
        in_specs=[pl.BlockSpec((tm, a.shape[1]), row), pl.BlockSpec((tm, d), row),
                  _resident(w.shape), _resident((1, d))],
        out_specs=pl.BlockSpec((tm, d), row),
        out_shape=jax.ShapeDtypeStruct((s, d), F32),
        compiler_params=_params(1),
        name="proj_residual",
    )(a, h, w, post_g)


def _tiles(s, d, f):
    ffn_cols = 512 if f % 512 == 0 else 256
    return dict(
        proj_rows=min(1024, s), proj_cols=min(512, d), qkv_cols=min(1024, d),
        conv_rows=min(256, s),
        ffn_rows=min(1024, s), ffn_cols=min(ffn_cols, f),
        ple_rows=min(512, s),
        attn_rows=min(1024, s),
        out_rows=min(512, s),
    )


def kernel(x, p, pre_mix_g, post_mix_g, conv_w_pw1, conv_b_pw1, conv_w_dw, conv_b_dw, conv_ln_g, conv_ln_b, conv_w_pw2, conv_b_pw2, attn_w_qkv, attn_lambda_q1, attn_lambda_k1, attn_lambda_q2, attn_lambda_k2, attn_subln_g, attn_w_o, pre_ffn_g, post_ffn_g, ffn_w_in, ffn_w_dw, ffn_w_out, ple_pre_g, ple_w_gate, ple_b_gate, ple_w_proj, ple_post_g):
    batch, s, d = x.shape
    depth = p.shape[0]
    f = ffn_w_out.shape[1]
    t = _tiles(s, d, f)
    vec = lambda a: a.reshape(1, -1)
    p_all = p.reshape(depth * batch, s, p.shape[-1])
    first_proj = lambda i: (conv_w_pw1, i // 2) if i % 2 == 0 else (attn_w_qkv, i // 2)
    outs = []
    for b in range(batch):
        h = x.reshape(s, d) if batch == 1 else x[b]
        w_first = first_proj(0)[0][0]
        for i in range(depth):
            j = i // 2
            layer_casts = [(ffn_w_in, i), (ffn_w_out, i), (ple_w_gate, i), (ple_w_proj, i)]
            if i % 2 == 0:
                u, w_pw2_bf = _norm_glu(h, vec(pre_mix_g[i]), w_first, vec(conv_b_pw1[j]),
                                        [(conv_w_pw2, j)], t["proj_rows"], t["proj_cols"])
                h, w_in_bf, w_out_bf, w_gate_bf, w_proj_bf = _conv_mix(
                    u, h, conv_w_dw[j], vec(conv_b_dw[j]), vec(conv_ln_g[j]), vec(conv_ln_b[j]),
                    w_pw2_bf, vec(conv_b_pw2[j]), vec(post_mix_g[i]), layer_casts, t["conv_rows"])
            else:
                q_scale = HEAD_DIM ** -0.5 * math.log2(math.e)
                col_scale = jnp.concatenate([jnp.full((1, d), q_scale, F32),
                                             jnp.ones((1, 2 * d), F32)], axis=1)
                qkv, w_o_bf = _norm_matmul(h, vec(pre_mix_g[i]), w_first, col_scale,
                                           [(attn_w_o, j)], t["proj_rows"], t["qkv_cols"], BF16)
                lambda_init = 0.8 - 0.6 * math.exp(-0.3 * i)
                o, w_in_bf, w_out_bf, w_gate_bf, w_proj_bf = _diff_attention(
                    qkv, vec(attn_lambda_q1[j]), vec(attn_lambda_k1[j]), vec(attn_lambda_q2[j]),
                    vec(attn_lambda_k2[j]), vec(attn_subln_g[j]), lambda_init,
                    layer_casts, t["attn_rows"])
                h = _proj_residual(o, h, w_o_bf, vec(post_mix_g[i]), t["out_rows"])
            h = _conv_ffn(h, vec(pre_ffn_g[i]), w_in_bf, ffn_w_dw, w_out_bf, i,
                          vec(post_ffn_g[i]), t["ffn_rows"], t["ffn_cols"])
            next_casts = [first_proj(i + 1)] if i + 1 < depth else []
            h, *w_next = _per_layer_embed(h, p_all, i * batch + b, vec(ple_pre_g[i]), w_gate_bf,
                                          vec(ple_b_gate[i]), w_proj_bf, vec(ple_post_g[i]),
                                          next_casts, t["ple_rows"])
            w_first = w_next[0] if w_next else None
        outs.append(h)
    return outs[0].reshape(1, s, d) if batch == 1 else jnp.stack(outs)
```

```python
import functools
import math

import jax
import jax.numpy as jnp
from jax import lax
from jax.experimental import pallas as pl
from jax.experimental.pallas import tpu as pltpu

F32 = jnp.float32
BF16 = jnp.bfloat16

RMS_EPS = 1e-6
LN_EPS = 1e-5
MASK_VALUE = -1e30
HEAD_DIM = 128
CONV_HALO = 32
CONV_LANE_CHUNKS = 4
SUBLANES = 8
LANES = 128
V7X_VMEM_LIMIT_BYTES = 56 * 1024 * 1024


def _params(n_axes, vmem_bytes=V7X_VMEM_LIMIT_BYTES):
    return pltpu.CompilerParams(dimension_semantics=("arbitrary",) * n_axes,
                                vmem_limit_bytes=vmem_bytes)


def _resident(shape):
    return pl.BlockSpec(shape, lambda *_: (0,) * len(shape), pipeline_mode=pl.Buffered(1))


def _cast_sidecar(w, layer, n_steps, step_of):
    rows, cols = w.shape[1:]
    bf16_rows = 2 * SUBLANES
    n_blocks = max(nb for nb in range(1, n_steps + 1)
                   if n_steps % nb == 0 and rows % nb == 0 and (rows // nb) % bf16_rows == 0)
    per, blk = n_steps // n_blocks, rows // n_blocks
    in_spec = pl.BlockSpec((None, blk, cols), lambda *g: (layer, step_of(*g) // per, 0))
    out_spec = pl.BlockSpec((blk, cols), lambda *g: (step_of(*g) // per, 0))
    return in_spec, out_spec, jax.ShapeDtypeStruct((rows, cols), BF16), per


def _host_call(body, casts, grid, in_specs, out_spec, out_shape, scratch_shapes, name, args):
    n_steps = math.prod(grid)

    def step_of(*g):
        step = 0
        for idx, extent in zip(g, grid):
            step = step * extent + idx
        return step

    sidecars = [_cast_sidecar(w, layer, n_steps, step_of) for w, layer in casts]
    n_in, n_cast = len(in_specs), len(casts)

    def kernel(*refs):
        ins, cast_ins = refs[:n_in], refs[n_in:n_in + n_cast]
        out, cast_outs = refs[n_in + n_cast], refs[n_in + n_cast + 1:n_in + 2 * n_cast + 1]
        scratch = refs[n_in + 2 * n_cast + 1:]
        step = step_of(*(pl.program_id(a) for a in range(len(grid))))
        for (_, _, _, per), src, dst in zip(sidecars, cast_ins, cast_outs):
            @pl.when(step % per == 0)
            def _():
                dst[...] = src[...].astype(dst.dtype)
        body(*ins, out, *scratch)

    return pl.pallas_call(
        kernel,
        grid=grid,
        in_specs=list(in_specs) + [sc[0] for sc in sidecars],
        out_specs=(out_spec,) + tuple(sc[1] for sc in sidecars),
        out_shape=(out_shape,) + tuple(sc[2] for sc in sidecars),
        scratch_shapes=scratch_shapes,
        compiler_params=_params(len(grid)),
        name=name,
    )(*args, *(w for w, _ in casts))


def _rms_rows(x, g):
    ms = jnp.mean(x * x, axis=-1, keepdims=True)
    return x * lax.rsqrt(ms + RMS_EPS) * g


def _sigmoid(x):
    return 1.0 / (1.0 + jnp.exp(-x))


def _dot(a, b):
    return jnp.dot(a, b, preferred_element_type=F32)


def _as_bf16(w):
    return w if w.dtype == BF16 else w.astype(BF16)


def _norm_to_scratch(dst_ref, src_ref, g, rows, dst_off=0, chunk=256):
    chunk = min(chunk, rows)

    def body(c, carry):
        r = pl.multiple_of(c * chunk, chunk)
        x = src_ref[pl.ds(r, chunk), :]
        dst_ref[pl.ds(dst_off + r, chunk), :] = _rms_rows(x, g).astype(dst_ref.dtype)
        return carry

    lax.fori_loop(0, rows // chunk, body, 0)


def _norm_glu_kernel(x_ref, g_ref, wa_ref, wg_ref, ba_ref, bg_ref, o_ref, xn_ref):
    @pl.when(pl.program_id(1) == 0)
    def _():
        _norm_to_scratch(xn_ref, x_ref, g_ref[...], x_ref.shape[0])

    xn = xn_ref[...]
    a = _dot(xn, _as_bf16(wa_ref[...])) + ba_ref[...]
    gate = _dot(xn, _as_bf16(wg_ref[...])) + bg_ref[...]
    o_ref[...] = (a * _sigmoid(gate)).astype(o_ref.dtype)


def _norm_glu(x, g, w, b, casts, tm, tn):
    s, d = x.shape
    n = w.shape[1] // 2
    nj = n // tn
    return _host_call(
        _norm_glu_kernel, casts,
        grid=(s // tm, nj),
        in_specs=[
            pl.BlockSpec((tm, d), lambda i, j: (i, 0)),
            pl.BlockSpec((1, d), lambda i, j: (0, 0)),
            pl.BlockSpec((d, tn), lambda i, j: (0, j)),
            pl.BlockSpec((d, tn), lambda i, j: (0, j + nj)),
            pl.BlockSpec((1, tn), lambda i, j: (0, j)),
            pl.BlockSpec((1, tn), lambda i, j: (0, j + nj)),
        ],
        out_spec=pl.BlockSpec((tm, tn), lambda i, j: (i, j)),
        out_shape=jax.ShapeDtypeStruct((s, n), F32),
        scratch_shapes=[pltpu.VMEM((tm, d), BF16)],
        name="norm_glu",
        args=(x, g, w, w, b, b))


def _norm_matmul_kernel(x_ref, g_ref, w_ref, cs_ref, o_ref, xn_ref):
    @pl.when(pl.program_id(1) == 0)
    def _():
        _norm_to_scratch(xn_ref, x_ref, g_ref[...], x_ref.shape[0])

    o_ref[...] = (_dot(xn_ref[...], _as_bf16(w_ref[...])) * cs_ref[...]).astype(o_ref.dtype)


def _norm_matmul(x, g, w, col_scale, casts, tm, tn, out_dtype):
    s, d = x.shape
    n = w.shape[1]
    return _host_call(
        _norm_matmul_kernel, casts,
        grid=(s // tm, n // tn),
        in_specs=[
            pl.BlockSpec((tm, d), lambda i, j: (i, 0)),
            pl.BlockSpec((1, d), lambda i, j: (0, 0)),
            pl.BlockSpec((d, tn), lambda i, j: (0, j)),
            pl.BlockSpec((1, tn), lambda i, j: (0, j)),
        ],
        out_spec=pl.BlockSpec((tm, tn), lambda i, j: (i, j)),
        out_shape=jax.ShapeDtypeStruct((s, n), out_dtype),
        scratch_shapes=[pltpu.VMEM((tm, d), BF16)],
        name="norm_matmul",
        args=(x, g, w, col_scale))


def _dwconv_window(win, w8_ref, lanes, rows, k):
    first = CONV_HALO - (k - 1)
    n_win = rows + CONV_HALO
    acc = None
    for b in range(SUBLANES):
        offs = [o for o in range(b, CONV_HALO + 1, SUBLANES) if 0 <= o - first < k]
        if not offs:
            continue
        shifted = win if b == 0 else pltpu.roll(win, n_win - b, axis=0)
        for o in offs:
            tap = pltpu.repeat(w8_ref[o - first, :, lanes], rows // SUBLANES, axis=0)
            term = shifted[o - b:o - b + rows] * tap
            acc = term if acc is None else acc + term
    return acc


def _conv_mix_kernel(u_ref, halo_ref, h_ref, wdw_ref, bdw_ref, lng_ref, lnb_ref, w2_ref, b2_ref,
                     pg_ref, o_ref, ext_ref, act_ref, w8_ref, *, k, rows):
    tm, d = u_ref.shape
    ext_ref[0:CONV_HALO, :] = jnp.where(pl.program_id(0) > 0, halo_ref[...], 0.0)
    ext_ref[CONV_HALO:, :] = u_ref[...]
    for j in range(k):
        w8_ref[j] = jnp.broadcast_to(wdw_ref[j:j + 1, :], (SUBLANES, d))

    n_lane = d // LANES

    groups = n_lane // CONV_LANE_CHUNKS

    def conv_body(c, carry):
        r0 = pl.multiple_of((c // groups) * rows, rows)
        for chunk_in_group in range(CONV_LANE_CHUNKS):
            lane0 = ((c % groups) * CONV_LANE_CHUNKS + chunk_in_group) * LANES
            lanes = pl.ds(pl.multiple_of(lane0, LANES), LANES)
            win = ext_ref[pl.ds(r0, rows + CONV_HALO), lanes]
            y = _dwconv_window(win, w8_ref, lanes, rows, k) + bdw_ref[:, lanes]
            ext_ref[pl.ds(r0, rows), lanes] = y
        return carry

    lax.fori_loop(0, (tm // rows) * groups, conv_body, 0)

    chunk = min(256, tm)

    def ln_body(c, carry):
        r = pl.multiple_of(c * chunk, chunk)
        x = ext_ref[pl.ds(r, chunk), :]
        mu = jnp.mean(x, axis=-1, keepdims=True)
        xc = x - mu
        y = xc * lax.rsqrt(jnp.mean(xc * xc, axis=-1, keepdims=True) + LN_EPS)
        y = y * lng_ref[...] + lnb_ref[...]
        act_ref[pl.ds(r, chunk), :] = (y * _sigmoid(y)).astype(act_ref.dtype)
        return carry

    lax.fori_loop(0, tm // chunk, ln_body, 0)

    y = _dot(act_ref[...], w2_ref[...]) + b2_ref[...]
    o_ref[...] = h_ref[...] + _rms_rows(y, pg_ref[...])


def _conv_mix(u, h, w_dw, b_dw, ln_g, ln_b, w2, b2, post_g, casts, tm):
    s, d = u.shape
    k = w_dw.shape[0]
    assert d % (CONV_LANE_CHUNKS * LANES) == 0 and tm % CONV_HALO == 0
    halo_blocks = tm // CONV_HALO
    row = lambda i: (i, 0)
    return _host_call(
        functools.partial(_conv_mix_kernel, k=k, rows=min(128, tm)), casts,
        grid=(s // tm,),
        in_specs=[
            pl.BlockSpec((tm, d), row),
            pl.BlockSpec((CONV_HALO, d), lambda i: (jnp.maximum(i * halo_blocks - 1, 0), 0)),
            pl.BlockSpec((tm, d), row),
            _resident((k, d)),
            _resident((1, d)),
            _resident((1, d)),
            _resident((1, d)),
            _resident((d, d)),
            _resident((1, d)),
            _resident((1, d)),
        ],
        out_spec=pl.BlockSpec((tm, d), row),
        out_shape=jax.ShapeDtypeStruct((s, d), F32),
        scratch_shapes=[pltpu.VMEM((tm + CONV_HALO, d), F32), pltpu.VMEM((tm, d), BF16),
                        pltpu.VMEM((k, SUBLANES, d), F32)],
        name="conv_mix",
        args=(u, u, h, w_dw, b_dw, ln_g, ln_b, w2, b2, post_g))


def _causal3(u, tail, w):
    tm = u.shape[0]
    ext = jnp.concatenate([tail, u], axis=0)
    return (u * w[2:3]
            + ext[SUBLANES - 1:SUBLANES - 1 + tm] * w[1:2]
            + ext[SUBLANES - 2:SUBLANES - 2 + tm] * w[0:1])


def _ffn_kernel(h_ref, g_ref, wg_ref, wu_ref, dwg_ref, dwu_ref, wo_ref, pg_ref,
                o_ref, xn_ref, tail_ref):
    i = pl.program_id(0)
    j = pl.program_id(1)
    tm = h_ref.shape[0]

    @pl.when(j == 0)
    def _():
        _norm_to_scratch(xn_ref, h_ref, g_ref[...], tm)
        o_ref[...] = jnp.zeros(o_ref.shape, o_ref.dtype)

    @pl.when(i == 0)
    def _():
        tail_ref[j] = jnp.zeros(tail_ref.shape[1:], F32)

    xn = xn_ref[...]
    pre_gate = _dot(xn, wg_ref[...])
    pre_up = _dot(xn, wu_ref[...])
    gate = _causal3(pre_gate, tail_ref[j, 0], dwg_ref[...])
    up = _causal3(pre_up, tail_ref[j, 1], dwu_ref[...])
    tail_ref[j, 0] = pre_gate[tm - SUBLANES:]
    tail_ref[j, 1] = pre_up[tm - SUBLANES:]
    act = (gate * _sigmoid(gate) * up).astype(BF16)
    o_ref[...] += _dot(act, wo_ref[...])

    @pl.when(j == pl.num_programs(1) - 1)
    def _():
        chunk = min(256, tm)

        def body(c, carry):
            r = pl.multiple_of(c * chunk, chunk)
            rows = pl.ds(r, chunk)
            o_ref[rows, :] = h_ref[rows, :] + _rms_rows(o_ref[rows, :], pg_ref[...])
            return carry

        lax.fori_loop(0, tm // chunk, body, 0)


def _conv_ffn(h, pre_g, w_in, w_dw, w_out, layer, post_g, tm, tf):
    s, d = h.shape
    f = w_out.shape[0]
    nj = f // tf
    return pl.pallas_call(
        _ffn_kernel,
        grid=(s // tm, nj),
        in_specs=[
            pl.BlockSpec((tm, d), lambda i, j: (i, 0), pipeline_mode=pl.Buffered(1)),
            pl.BlockSpec((1, d), lambda i, j: (0, 0)),
            pl.BlockSpec((d, tf), lambda i, j: (0, j)),
            pl.BlockSpec((d, tf), lambda i, j: (0, j + nj)),
            pl.BlockSpec((None, 3, tf), lambda i, j: (layer, 0, j)),
            pl.BlockSpec((None, 3, tf), lambda i, j: (layer, 0, j + nj)),
            pl.BlockSpec((tf, d), lambda i, j: (j, 0)),
            pl.BlockSpec((1, d), lambda i, j: (0, 0)),
        ],
        out_specs=pl.BlockSpec((tm, d), lambda i, j: (i, 0)),
        out_shape=jax.ShapeDtypeStruct((s, d), F32),
        scratch_shapes=[pltpu.VMEM((tm, d), BF16), pltpu.VMEM((nj, 2, SUBLANES, tf), F32)],
        compiler_params=_params(2),
        name="conv_ffn",
    )(h, pre_g, w_in, w_in, w_dw, w_dw, w_out, post_g)


def _ple_kernel(h_ref, p_ref, g_ref, wg_ref, bg_ref, wp_ref, pg_ref, o_ref):
    h = h_ref[...]
    xn = _rms_rows(h, g_ref[...]).astype(BF16)
    gate = _sigmoid(_dot(xn, wg_ref[...]) + bg_ref[...])
    emb = _dot(p_ref[...].astype(BF16), wp_ref[...])
    o_ref[...] = h + _rms_rows(gate * emb, pg_ref[...])


def _per_layer_embed(h, p, p_index, pre_g, w_gate, b_gate, w_proj, post_g, casts, tm):
    s, d = h.shape
    e = p.shape[2]
    row = lambda i: (i, 0)
    return _host_call(
        _ple_kernel, casts,
        grid=(s // tm,),
        in_specs=[
            pl.BlockSpec((tm, d), row),
            pl.BlockSpec((None, tm, e), lambda i: (p_index, i, 0)),
            _resident((1, d)),
            _resident((d, d)),
            _resident((1, d)),
            _resident((e, d)),
            _resident((1, d)),
        ],
        out_spec=pl.BlockSpec((tm, d), row),
        out_shape=jax.ShapeDtypeStruct((s, d), F32),
        scratch_shapes=[],
        name="per_layer_embed",
        args=(h, p, pre_g, w_gate, b_gate, w_proj, post_g))


L_FLOOR = 2.0 ** -64
TILES_PER_TRIP = 4


def _diff_attn_kernel(q_ref, k_ref, v_ref, lq1_ref, lk1_ref, lq2_ref, lk2_ref, sg_ref, o_ref,
                      b_ref, l_ref, acc_ref, kn_ref, *, lambda_init):
    qi = pl.program_id(1)
    tq = q_ref.shape[0]
    tk = tq
    n_lane_tiles = tk // LANES
    n_keys = k_ref.shape[0]

    def map_lanes(c):
        return slice(c * HEAD_DIM, (c + 1) * HEAD_DIM)

    def scores(k0, c, diagonal):
        s = lax.dot_general(q_ref[:, map_lanes(c)], k_ref[pl.ds(k0, tk), map_lanes(c)],
                            (((1,), (1,)), ((), ())), preferred_element_type=F32)
        if diagonal:
            rows = lax.broadcasted_iota(jnp.int32, s.shape, 0)
            cols = lax.broadcasted_iota(jnp.int32, s.shape, 1)
            s = jnp.where(rows >= cols, s, MASK_VALUE)
        return s

    @pl.when(qi == 0)
    def _():
        chunk = min(512, n_keys)
        for c in range(2):
            def body(i, best):
                rows = pl.ds(pl.multiple_of(i * chunk, chunk), chunk)
                kk = k_ref[rows, map_lanes(c)].astype(F32)
                return jnp.maximum(best, jnp.sum(kk * kk, axis=1, keepdims=True))

            best = lax.fori_loop(0, n_keys // chunk, body, jnp.zeros((chunk, 1), F32))
            kn_ref[c] = jnp.broadcast_to(jnp.sqrt(jnp.max(best, axis=0, keepdims=True)),
                                         (SUBLANES, LANES))

    def block(row0, n_rows, k0, n_keys_blk, triangle):
        rows = slice(row0, row0 + n_rows)
        keys = pl.ds(k0, n_keys_blk)
        v = v_ref[keys, :]
        for c in range(2):
            s = lax.dot_general(q_ref[rows, map_lanes(c)], k_ref[keys, map_lanes(c)],
                                (((1,), (1,)), ((), ())), preferred_element_type=F32)
            if triangle:
                r_idx = lax.broadcasted_iota(jnp.int32, s.shape, 0)
                c_idx = lax.broadcasted_iota(jnp.int32, s.shape, 1)
                s = jnp.where(r_idx >= c_idx, s, MASK_VALUE)
            p = jnp.exp2(s - pltpu.repeat(b_ref[c, rows, :], n_keys_blk // LANES, axis=1))
            psum = p[:, 0:LANES]
            for t in range(1, n_keys_blk // LANES):
                psum = psum + p[:, t * LANES:(t + 1) * LANES]
            l_ref[c, rows, :] += psum
            acc_ref[c, rows, :] += _dot(p.astype(BF16), v)

    def tile(k0, diagonal):
        if not diagonal:
            block(0, tq, k0, tk, False)
        else:
            half = tk // 2
            block(0, tq, k0, half, True)
            block(half, tq - half, pl.multiple_of(k0 + half, half), half, True)

    def start(t):
        return pl.multiple_of(t * tk, tk)

    def sweep():
        l_ref[...] = jnp.zeros(l_ref.shape, F32)
        acc_ref[...] = jnp.zeros(acc_ref.shape, F32)

        def trip(g, carry):
            for u in range(TILES_PER_TRIP):
                tile(start(TILES_PER_TRIP * g + u), False)
            return carry

        lax.fori_loop(0, qi // TILES_PER_TRIP, trip, 0)
        base = (qi // TILES_PER_TRIP) * TILES_PER_TRIP
        rem = qi % TILES_PER_TRIP

        @pl.when(rem >= 2)
        def _():
            tile(start(base), False)
            tile(start(base + 1), False)

        last = base + jnp.where(rem >= 2, 2, 0)

        @pl.when(rem % 2 == 1)
        def _():
            tile(start(last), False)
            tile(start(qi), True)

        @pl.when(rem % 2 == 0)
        def _():
            tile(start(qi), True)

    def exact_row_max():
        for c in range(2):
            def fold(best, s):
                for u in range(n_lane_tiles):
                    best = jnp.maximum(best, s[:, u * LANES:(u + 1) * LANES])
                return best

            best = lax.fori_loop(0, qi, lambda t, best: fold(best, scores(start(t), c, False)),
                                 jnp.full((tq, LANES), MASK_VALUE, F32))
            best = fold(best, scores(start(qi), c, True))
            b_ref[c] = jnp.broadcast_to(jnp.max(best, axis=1, keepdims=True), (tq, LANES))

    for c in range(2):
        qq = q_ref[:, map_lanes(c)].astype(F32)
        b_ref[c] = jnp.sqrt(jnp.sum(qq * qq, axis=1, keepdims=True)) * kn_ref[c, 0:1, :]

    def one_pass(state):
        attempt, _ = state

        @pl.when(attempt == 1)
        def _():
            exact_row_max()

        sweep()
        l_min = jnp.min(jnp.minimum(jnp.sum(l_ref[0], axis=1, keepdims=True),
                                    jnp.sum(l_ref[1], axis=1, keepdims=True)))
        done = jnp.logical_or(l_min >= L_FLOOR, attempt == 1)
        return attempt + 1, done.astype(jnp.int32)

    lax.while_loop(lambda state: state[1] == 0, one_pass, (jnp.int32(0), jnp.int32(0)))

    lam = (jnp.exp(jnp.sum(lq1_ref[...] * lk1_ref[...], keepdims=True))
           - jnp.exp(jnp.sum(lq2_ref[...] * lk2_ref[...], keepdims=True)) + lambda_init)
    l0 = jnp.sum(l_ref[0], axis=1, keepdims=True)
    l1 = jnp.sum(l_ref[1], axis=1, keepdims=True)
    o = acc_ref[0] / l0 - lam * (acc_ref[1] / l1)
    o_ref[...] = (_rms_rows(o, sg_ref[...]) * (1.0 - lambda_init)).astype(o_ref.dtype)


def _diff_attention(qkv, lq1, lk1, lq2, lk2, subln_g, lambda_init, casts, tq):
    s = qkv.shape[0]
    vd = 2 * HEAD_DIM
    heads = qkv.shape[1] // (3 * vd)
    vec = pl.BlockSpec((1, HEAD_DIM), lambda h, i: (0, 0))
    return _host_call(
        functools.partial(_diff_attn_kernel, lambda_init=lambda_init), casts,
        grid=(heads, s // tq),
        in_specs=[
            pl.BlockSpec((tq, vd), lambda h, i: (i, h)),
            pl.BlockSpec((s, vd), lambda h, i: (0, heads + h)),
            pl.BlockSpec((s, vd), lambda h, i: (0, 2 * heads + h)),
            vec, vec, vec, vec,
            pl.BlockSpec((1, vd), lambda h, i: (0, 0)),
        ],
        out_spec=pl.BlockSpec((tq, vd), lambda h, i: (i, h)),
        out_shape=jax.ShapeDtypeStruct((s, heads * vd), BF16),
        scratch_shapes=[pltpu.VMEM((2, tq, LANES), F32), pltpu.VMEM((2, tq, LANES), F32),
                        pltpu.VMEM((2, tq, vd), F32), pltpu.VMEM((2, SUBLANES, LANES), F32)],
        name="diff_attention",
        args=(qkv, qkv, qkv, lq1, lk1, lq2, lk2, subln_g))


def _proj_residual_kernel(a_ref, h_ref, w_ref, pg_ref, o_ref):
    o_ref[...] = h_ref[...] + _rms_rows(_dot(a_ref[...], w_ref[...]), pg_ref[...])


def _proj_residual(a, h, w, post_g, tm):
    s, d = h.shape
    row = lambda i: (i, 0)
    return pl.pallas_call(
        _proj_residual_kernel,
        grid=(s // tm,),
        in_specs=[pl.BlockSpec((tm, a.shape[1]), row), pl.BlockSpec((tm, d), row),
                  _resident(w.shape), _resident((1, d))],
        out_specs=pl.BlockSpec((tm, d), row),
        out_shape=jax.ShapeDtypeStruct((s, d), F32),
        compiler_params=_params(1),
        name="proj_residual",
    )(a, h, w, post_g)


def _tiles(s, d, f):
    ffn_cols = 512 if f % 512 == 0 else 256
    return dict(
        proj_rows=min(1024, s), proj_cols=min(512, d), qkv_cols=min(1024, d),
        conv_rows=min(256, s),
        ffn_rows=min(1024, s), ffn_cols=min(ffn_cols, f),
        ple_rows=min(512, s),
        attn_rows=min(1024, s),
        out_rows=min(512, s),
    )


def kernel(x, p, pre_mix_g, post_mix_g, conv_w_pw1, conv_b_pw1, conv_w_dw, conv_b_dw, conv_ln_g, conv_ln_b, conv_w_pw2, conv_b_pw2, attn_w_qkv, attn_lambda_q1, attn_lambda_k1, attn_lambda_q2, attn_lambda_k2, attn_subln_g, attn_w_o, pre_ffn_g, post_ffn_g, ffn_w_in, ffn_w_dw, ffn_w_out, ple_pre_g, ple_w_gate, ple_b_gate, ple_w_proj, ple_post_g):
    batch, s, d = x.shape
    depth = p.shape[0]
    f = ffn_w_out.shape[1]
    t = _tiles(s, d, f)
    vec = lambda a: a.reshape(1, -1)
    p_all = p.reshape(depth * batch, s, p.shape[-1])
    first_proj = lambda i: (conv_w_pw1, i // 2) if i % 2 == 0 else (attn_w_qkv, i // 2)
    outs = []
    for b in range(batch):
        h = x.reshape(s, d) if batch == 1 else x[b]
        w_first = first_proj(0)[0][0]
        for i in range(depth):
            j = i // 2
            layer_casts = [(ffn_w_in, i), (ffn_w_out, i), (ple_w_gate, i), (ple_w_proj, i)]
            if i % 2 == 0:
                u, w_pw2_bf = _norm_glu(h, vec(pre_mix_g[i]), w_first, vec(conv_b_pw1[j]),
                                        [(conv_w_pw2, j)], t["proj_rows"], t["proj_cols"])
                h, w_in_bf, w_out_bf, w_gate_bf, w_proj_bf = _conv_mix(
                    u, h, conv_w_dw[j], vec(conv_b_dw[j]), vec(conv_ln_g[j]), vec(conv_ln_b[j]),
                    w_pw2_bf, vec(conv_b_pw2[j]), vec(post_mix_g[i]), layer_casts, t["conv_rows"])
            else:
                q_scale = HEAD_DIM ** -0.5 * math.log2(math.e)
                col_scale = jnp.concatenate([jnp.full((1, d), q_scale, F32),
                                             jnp.ones((1, 2 * d), F32)], axis=1)
                qkv, w_o_bf = _norm_matmul(h, vec(pre_mix_g[i]), w_first, col_scale,
                                           [(attn_w_o, j)], t["proj_rows"], t["qkv_cols"], BF16)
                lambda_init = 0.8 - 0.6 * math.exp(-0.3 * i)
                o, w_in_bf, w_out_bf, w_gate_bf, w_proj_bf = _diff_attention(
                    qkv, vec(attn_lambda_q1[j]), vec(attn_lambda_k1[j]), vec(attn_lambda_q2[j]),
                    vec(attn_lambda_k2[j]), vec(attn_subln_g[j]), lambda_init,
                    layer_casts, t["attn_rows"])
                h = _proj_residual(o, h, w_o_bf, vec(post_mix_g[i]), t["out_rows"])
            h = _conv_ffn(h, vec(pre_ffn_g[i]), w_in_bf, ffn_w_dw, w_out_bf, i,
                          vec(post_ffn_g[i]), t["ffn_rows"], t["ffn_cols"])
            next_casts = [first_proj(i + 1)] if i + 1 < depth else []
            h, *w_next = _per_layer_embed(h, p_all, i * batch + b, vec(ple_pre_g[i]), w_gate_bf,
                                          vec(ple_b_gate[i]), w_proj_bf, vec(ple_post_g[i]),
                                          next_casts, t["ple_rows"])
            w_first = w_next[0] if w_next else None
        outs.append(h)
    return outs[0].reshape(1, s, d) if batch == 1 else jnp.stack(outs)
```

```python
import functools
import math

import jax
import jax.numpy as jnp
from jax import lax
from jax.experimental import pallas as pl
from jax.experimental.pallas import tpu as pltpu

F32 = jnp.float32
BF16 = jnp.bfloat16

RMS_EPS = 1e-6
LN_EPS = 1e-5
MASK_VALUE = -1e30
HEAD_DIM = 128
CONV_HALO = 32
CONV_LANE_CHUNKS = 4
SUBLANES = 8
LANES = 128
V7X_VMEM_LIMIT_BYTES = 56 * 1024 * 1024


def _params(n_axes, vmem_bytes=V7X_VMEM_LIMIT_BYTES):
    return pltpu.CompilerParams(dimension_semantics=("arbitrary",) * n_axes,
                                vmem_limit_bytes=vmem_bytes)


def _resident(shape):
    return pl.BlockSpec(shape, lambda *_: (0,) * len(shape), pipeline_mode=pl.Buffered(1))


def _cast_sidecar(w, layer, n_steps, step_of):
    rows, cols = w.shape[1:]
    bf16_rows = 2 * SUBLANES
    n_blocks = max(nb for nb in range(1, n_steps + 1)
                   if n_steps % nb == 0 and rows % nb == 0 and (rows // nb) % bf16_rows == 0)
    per, blk = n_steps // n_blocks, rows // n_blocks
    in_spec = pl.BlockSpec((None, blk, cols), lambda *g: (layer, step_of(*g) // per, 0))
    out_spec = pl.BlockSpec((blk, cols), lambda *g: (step_of(*g) // per, 0))
    return in_spec, out_spec, jax.ShapeDtypeStruct((rows, cols), BF16), per


def _host_call(body, casts, grid, in_specs, out_spec, out_shape, scratch_shapes, name, args):
    n_steps = math.prod(grid)

    def step_of(*g):
        step = 0
        for idx, extent in zip(g, grid):
            step = step * extent + idx
        return step

    sidecars = [_cast_sidecar(w, layer, n_steps, step_of) for w, layer in casts]
    out_specs = out_spec if isinstance(out_spec, tuple) else (out_spec,)
    out_shapes = out_shape if isinstance(out_shape, tuple) else (out_shape,)
    n_in, n_cast, n_out = len(in_specs), len(casts), len(out_specs)

    def kernel(*refs):
        ins, cast_ins = refs[:n_in], refs[n_in:n_in + n_cast]
        outs = refs[n_in + n_cast:n_in + n_cast + n_out]
        cast_outs = refs[n_in + n_cast + n_out:n_in + 2 * n_cast + n_out]
        scratch = refs[n_in + 2 * n_cast + n_out:]
        step = step_of(*(pl.program_id(a) for a in range(len(grid))))
        for (_, _, _, per), src, dst in zip(sidecars, cast_ins, cast_outs):
            @pl.when(step % per == 0)
            def _():
                dst[...] = src[...].astype(dst.dtype)
        body(*ins, *outs, *scratch)

    return pl.pallas_call(
        kernel,
        grid=grid,
        in_specs=list(in_specs) + [sc[0] for sc in sidecars],
        out_specs=out_specs + tuple(sc[1] for sc in sidecars),
        out_shape=out_shapes + tuple(sc[2] for sc in sidecars),
        scratch_shapes=scratch_shapes,
        compiler_params=_params(len(grid)),
        name=name,
    )(*args, *(w for w, _ in casts))


def _rms_rows(x, g):
    ms = jnp.mean(x * x, axis=-1, keepdims=True)
    return x * lax.rsqrt(ms + RMS_EPS) * g


def _sigmoid(x):
    return 1.0 / (1.0 + jnp.exp(-x))


def _dot(a, b):
    return jnp.dot(a, b, preferred_element_type=F32)


def _as_bf16(w):
    return w if w.dtype == BF16 else w.astype(BF16)


def _norm_to_scratch(dst_ref, src_ref, g, rows, dst_off=0, chunk=256):
    chunk = min(chunk, rows)

    def body(c, carry):
        r = pl.multiple_of(c * chunk, chunk)
        x = src_ref[pl.ds(r, chunk), :]
        dst_ref[pl.ds(dst_off + r, chunk), :] = _rms_rows(x, g).astype(dst_ref.dtype)
        return carry

    lax.fori_loop(0, rows // chunk, body, 0)


def _norm_glu_kernel(x_ref, g_ref, wa_ref, wg_ref, ba_ref, bg_ref, o_ref, xn_ref):
    @pl.when(pl.program_id(1) == 0)
    def _():
        _norm_to_scratch(xn_ref, x_ref, g_ref[...], x_ref.shape[0])

    xn = xn_ref[...]
    a = _dot(xn, _as_bf16(wa_ref[...])) + ba_ref[...]
    gate = _dot(xn, _as_bf16(wg_ref[...])) + bg_ref[...]
    o_ref[...] = (a * _sigmoid(gate)).astype(o_ref.dtype)


def _norm_glu(x, g, w, b, casts, tm, tn):
    s, d = x.shape
    n = w.shape[1] // 2
    nj = n // tn
    return _host_call(
        _norm_glu_kernel, casts,
        grid=(s // tm, nj),
        in_specs=[
            pl.BlockSpec((tm, d), lambda i, j: (i, 0)),
            pl.BlockSpec((1, d), lambda i, j: (0, 0)),
            pl.BlockSpec((d, tn), lambda i, j: (0, j)),
            pl.BlockSpec((d, tn), lambda i, j: (0, j + nj)),
            pl.BlockSpec((1, tn), lambda i, j: (0, j)),
            pl.BlockSpec((1, tn), lambda i, j: (0, j + nj)),
        ],
        out_spec=pl.BlockSpec((tm, tn), lambda i, j: (i, j)),
        out_shape=jax.ShapeDtypeStruct((s, n), F32),
        scratch_shapes=[pltpu.VMEM((tm, d), BF16)],
        name="norm_glu",
        args=(x, g, w, w, b, b))


def _norm_matmul_kernel(x_ref, g_ref, w_ref, cs_ref, o_ref, xn_ref):
    @pl.when(pl.program_id(1) == 0)
    def _():
        _norm_to_scratch(xn_ref, x_ref, g_ref[...], x_ref.shape[0])

    o_ref[...] = (_dot(xn_ref[...], _as_bf16(w_ref[...])) * cs_ref[...]).astype(o_ref.dtype)


def _scaled_matmul_kernel(x_ref, w_ref, cs_ref, o_ref):
    o_ref[...] = (_dot(x_ref[...], _as_bf16(w_ref[...])) * cs_ref[...]).astype(o_ref.dtype)


def _scaled_matmul(xn, w, col_scale, casts, tm, tn, out_dtype):
    s, d = xn.shape
    n = w.shape[1]
    return _host_call(
        _scaled_matmul_kernel, casts,
        grid=(s // tm, n // tn),
        in_specs=[
            pl.BlockSpec((tm, d), lambda i, j: (i, 0)),
            pl.BlockSpec((d, tn), lambda i, j: (0, j)),
            pl.BlockSpec((1, tn), lambda i, j: (0, j)),
        ],
        out_spec=pl.BlockSpec((tm, tn), lambda i, j: (i, j)),
        out_shape=jax.ShapeDtypeStruct((s, n), out_dtype),
        scratch_shapes=[],
        name="scaled_matmul",
        args=(xn, w, col_scale))


def _norm_matmul(x, g, w, col_scale, casts, tm, tn, out_dtype):
    s, d = x.shape
    n = w.shape[1]
    return _host_call(
        _norm_matmul_kernel, casts,
        grid=(s // tm, n // tn),
        in_specs=[
            pl.BlockSpec((tm, d), lambda i, j: (i, 0)),
            pl.BlockSpec((1, d), lambda i, j: (0, 0)),
            pl.BlockSpec((d, tn), lambda i, j: (0, j)),
            pl.BlockSpec((1, tn), lambda i, j: (0, j)),
        ],
        out_spec=pl.BlockSpec((tm, tn), lambda i, j: (i, j)),
        out_shape=jax.ShapeDtypeStruct((s, n), out_dtype),
        scratch_shapes=[pltpu.VMEM((tm, d), BF16)],
        name="norm_matmul",
        args=(x, g, w, col_scale))


def _dwconv_window(win, w8_ref, lanes, rows, k):
    first = CONV_HALO - (k - 1)
    n_win = rows + CONV_HALO
    acc = None
    for b in range(SUBLANES):
        offs = [o for o in range(b, CONV_HALO + 1, SUBLANES) if 0 <= o - first < k]
        if not offs:
            continue
        shifted = win if b == 0 else pltpu.roll(win, n_win - b, axis=0)
        for o in offs:
            tap = pltpu.repeat(w8_ref[o - first, :, lanes], rows // SUBLANES, axis=0)
            term = shifted[o - b:o - b + rows] * tap
            acc = term if acc is None else acc + term
    return acc


def _conv_mix_kernel(u_ref, halo_ref, h_ref, wdw_ref, bdw_ref, lng_ref, lnb_ref, w2_ref, b2_ref,
                     pg_ref, o_ref, ext_ref, act_ref, w8_ref, *, k, rows):
    tm, d = u_ref.shape
    ext_ref[0:CONV_HALO, :] = jnp.where(pl.program_id(0) > 0, halo_ref[...], 0.0)
    ext_ref[CONV_HALO:, :] = u_ref[...]
    for j in range(k):
        w8_ref[j] = jnp.broadcast_to(wdw_ref[j:j + 1, :], (SUBLANES, d))

    n_lane = d // LANES

    groups = n_lane // CONV_LANE_CHUNKS

    def conv_body(c, carry):
        r0 = pl.multiple_of((c // groups) * rows, rows)
        for chunk_in_group in range(CONV_LANE_CHUNKS):
            lane0 = ((c % groups) * CONV_LANE_CHUNKS + chunk_in_group) * LANES
            lanes = pl.ds(pl.multiple_of(lane0, LANES), LANES)
            win = ext_ref[pl.ds(r0, rows + CONV_HALO), lanes]
            y = _dwconv_window(win, w8_ref, lanes, rows, k) + bdw_ref[:, lanes]
            ext_ref[pl.ds(r0, rows), lanes] = y
        return carry

    lax.fori_loop(0, (tm // rows) * groups, conv_body, 0)

    chunk = min(256, tm)

    def ln_body(c, carry):
        r = pl.multiple_of(c * chunk, chunk)
        x = ext_ref[pl.ds(r, chunk), :]
        mu = jnp.mean(x, axis=-1, keepdims=True)
        xc = x - mu
        y = xc * lax.rsqrt(jnp.mean(xc * xc, axis=-1, keepdims=True) + LN_EPS)
        y = y * lng_ref[...] + lnb_ref[...]
        act_ref[pl.ds(r, chunk), :] = (y * _sigmoid(y)).astype(act_ref.dtype)
        return carry

    lax.fori_loop(0, tm // chunk, ln_body, 0)

    y = _dot(act_ref[...], w2_ref[...]) + b2_ref[...]
    o_ref[...] = h_ref[...] + _rms_rows(y, pg_ref[...])


def _conv_mix(u, h, w_dw, b_dw, ln_g, ln_b, w2, b2, post_g, casts, tm):
    s, d = u.shape
    k = w_dw.shape[0]
    assert d % (CONV_LANE_CHUNKS * LANES) == 0 and tm % CONV_HALO == 0
    halo_blocks = tm // CONV_HALO
    row = lambda i: (i, 0)
    return _host_call(
        functools.partial(_conv_mix_kernel, k=k, rows=min(128, tm)), casts,
        grid=(s // tm,),
        in_specs=[
            pl.BlockSpec((tm, d), row),
            pl.BlockSpec((CONV_HALO, d), lambda i: (jnp.maximum(i * halo_blocks - 1, 0), 0)),
            pl.BlockSpec((tm, d), row),
            _resident((k, d)),
            _resident((1, d)),
            _resident((1, d)),
            _resident((1, d)),
            _resident((d, d)),
            _resident((1, d)),
            _resident((1, d)),
        ],
        out_spec=pl.BlockSpec((tm, d), row),
        out_shape=jax.ShapeDtypeStruct((s, d), F32),
        scratch_shapes=[pltpu.VMEM((tm + CONV_HALO, d), F32), pltpu.VMEM((tm, d), BF16),
                        pltpu.VMEM((k, SUBLANES, d), F32)],
        name="conv_mix",
        args=(u, u, h, w_dw, b_dw, ln_g, ln_b, w2, b2, post_g))


def _causal3(u, tail, w):
    tm = u.shape[0]
    ext = jnp.concatenate([tail, u], axis=0)
    return (u * w[2:3]
            + ext[SUBLANES - 1:SUBLANES - 1 + tm] * w[1:2]
            + ext[SUBLANES - 2:SUBLANES - 2 + tm] * w[0:1])


def _ffn_kernel(h_ref, g_ref, wg_ref, wu_ref, dwg_ref, dwu_ref, wo_ref, pg_ref,
                o_ref, xn_ref, tail_ref):
    i = pl.program_id(0)
    j = pl.program_id(1)
    tm = h_ref.shape[0]

    @pl.when(j == 0)
    def _():
        _norm_to_scratch(xn_ref, h_ref, g_ref[...], tm)
        o_ref[...] = jnp.zeros(o_ref.shape, o_ref.dtype)

    @pl.when(i == 0)
    def _():
        tail_ref[j] = jnp.zeros(tail_ref.shape[1:], F32)

    xn = xn_ref[...]
    pre_gate = _dot(xn, wg_ref[...])
    pre_up = _dot(xn, wu_ref[...])
    gate = _causal3(pre_gate, tail_ref[j, 0], dwg_ref[...])
    up = _causal3(pre_up, tail_ref[j, 1], dwu_ref[...])
    tail_ref[j, 0] = pre_gate[tm - SUBLANES:]
    tail_ref[j, 1] = pre_up[tm - SUBLANES:]
    act = (gate * _sigmoid(gate) * up).astype(BF16)
    o_ref[...] += _dot(act, wo_ref[...])

    @pl.when(j == pl.num_programs(1) - 1)
    def _():
        chunk = min(256, tm)

        def body(c, carry):
            r = pl.multiple_of(c * chunk, chunk)
            rows = pl.ds(r, chunk)
            o_ref[rows, :] = h_ref[rows, :] + _rms_rows(o_ref[rows, :], pg_ref[...])
            return carry

        lax.fori_loop(0, tm // chunk, body, 0)


def _conv_ffn(h, pre_g, w_in, w_dw, w_out, layer, post_g, tm, tf):
    s, d = h.shape
    f = w_out.shape[0]
    nj = f // tf
    return pl.pallas_call(
        _ffn_kernel,
        grid=(s // tm, nj),
        in_specs=[
            pl.BlockSpec((tm, d), lambda i, j: (i, 0), pipeline_mode=pl.Buffered(1)),
            pl.BlockSpec((1, d), lambda i, j: (0, 0)),
            pl.BlockSpec((d, tf), lambda i, j: (0, j)),
            pl.BlockSpec((d, tf), lambda i, j: (0, j + nj)),
            pl.BlockSpec((None, 3, tf), lambda i, j: (layer, 0, j)),
            pl.BlockSpec((None, 3, tf), lambda i, j: (layer, 0, j + nj)),
            pl.BlockSpec((tf, d), lambda i, j: (j, 0)),
            pl.BlockSpec((1, d), lambda i, j: (0, 0)),
        ],
        out_specs=pl.BlockSpec((tm, d), lambda i, j: (i, 0)),
        out_shape=jax.ShapeDtypeStruct((s, d), F32),
        scratch_shapes=[pltpu.VMEM((tm, d), BF16), pltpu.VMEM((nj, 2, SUBLANES, tf), F32)],
        compiler_params=_params(2),
        name="conv_ffn",
    )(h, pre_g, w_in, w_in, w_dw, w_dw, w_out, post_g)


def _ple_kernel(h_ref, p_ref, g_ref, wg_ref, bg_ref, wp_ref, pg_ref, *rest):
    h = h_ref[...]
    xn = _rms_rows(h, g_ref[...]).astype(BF16)
    gate = _sigmoid(_dot(xn, wg_ref[...]) + bg_ref[...])
    emb = _dot(p_ref[...].astype(BF16), wp_ref[...])
    new_h = h + _rms_rows(gate * emb, pg_ref[...])
    if len(rest) == 1:
        rest[0][...] = new_h
    else:
        ng_ref, o_ref, xn_ref = rest
        o_ref[...] = new_h
        xn_ref[...] = _rms_rows(new_h, ng_ref[...]).astype(xn_ref.dtype)


def _per_layer_embed(h, p, p_index, pre_g, w_gate, b_gate, w_proj, post_g, next_g, casts, tm):
    s, d = h.shape
    e = p.shape[2]
    row = lambda i: (i, 0)
    block = pl.BlockSpec((tm, d), row)
    in_specs = [
        block,
        pl.BlockSpec((None, tm, e), lambda i: (p_index, i, 0)),
        _resident((1, d)),
        _resident((d, d)),
        _resident((1, d)),
        _resident((e, d)),
        _resident((1, d)),
    ]
    args = (h, p, pre_g, w_gate, b_gate, w_proj, post_g)
    out_spec, out_shape = block, jax.ShapeDtypeStruct((s, d), F32)
    if next_g is not None:
        in_specs, args = in_specs + [_resident((1, d))], args + (next_g,)
        out_spec = (block, block)
        out_shape = (out_shape, jax.ShapeDtypeStruct((s, d), BF16))
    return _host_call(_ple_kernel, casts, grid=(s // tm,), in_specs=in_specs, out_spec=out_spec,
                      out_shape=out_shape, scratch_shapes=[], name="per_layer_embed", args=args)


L_FLOOR = 2.0 ** -64
TILES_PER_TRIP = 4


def _diff_attn_kernel(q_ref, k_ref, v_ref, lq1_ref, lk1_ref, lq2_ref, lk2_ref, sg_ref, o_ref,
                      b_ref, l_ref, acc_ref, kn_ref, *, lambda_init):
    qi = pl.program_id(1)
    tq = q_ref.shape[0]
    tk = tq
    n_lane_tiles = tk // LANES
    n_keys = k_ref.shape[0]

    def map_lanes(c):
        return slice(c * HEAD_DIM, (c + 1) * HEAD_DIM)

    def scores(k0, c, diagonal):
        s = lax.dot_general(q_ref[:, map_lanes(c)], k_ref[pl.ds(k0, tk), map_lanes(c)],
                            (((1,), (1,)), ((), ())), preferred_element_type=F32)
        if diagonal:
            rows = lax.broadcasted_iota(jnp.int32, s.shape, 0)
            cols = lax.broadcasted_iota(jnp.int32, s.shape, 1)
            s = jnp.where(rows >= cols, s, MASK_VALUE)
        return s

    @pl.when(qi == 0)
    def _():
        chunk = min(512, n_keys)
        for c in range(2):
            def body(i, best):
                rows = pl.ds(pl.multiple_of(i * chunk, chunk), chunk)
                kk = k_ref[rows, map_lanes(c)].astype(F32)
                return jnp.maximum(best, jnp.sum(kk * kk, axis=1, keepdims=True))

            best = lax.fori_loop(0, n_keys // chunk, body, jnp.zeros((chunk, 1), F32))
            kn_ref[c] = jnp.broadcast_to(jnp.sqrt(jnp.max(best, axis=0, keepdims=True)),
                                         (SUBLANES, LANES))

    def block(row0, n_rows, k0, n_keys_blk, triangle):
        rows = slice(row0, row0 + n_rows)
        keys = pl.ds(k0, n_keys_blk)
        v = v_ref[keys, :]
        for c in range(2):
            s = lax.dot_general(q_ref[rows, map_lanes(c)], k_ref[keys, map_lanes(c)],
                                (((1,), (1,)), ((), ())), preferred_element_type=F32)
            if triangle:
                r_idx = lax.broadcasted_iota(jnp.int32, s.shape, 0)
                c_idx = lax.broadcasted_iota(jnp.int32, s.shape, 1)
                s = jnp.where(r_idx >= c_idx, s, MASK_VALUE)
            p = jnp.exp2(s - pltpu.repeat(b_ref[c, rows, :], n_keys_blk // LANES, axis=1))
            psum = p[:, 0:LANES]
            for t in range(1, n_keys_blk // LANES):
                psum = psum + p[:, t * LANES:(t + 1) * LANES]
            l_ref[c, rows, :] += psum
            acc_ref[c, rows, :] += _dot(p.astype(BF16), v)

    def tile(k0, diagonal):
        if not diagonal:
            block(0, tq, k0, tk, False)
        else:
            half = tk // 2
            block(0, tq, k0, half, True)
            block(half, tq - half, pl.multiple_of(k0 + half, half), half, True)

    def start(t):
        return pl.multiple_of(t * tk, tk)

    def sweep():
        l_ref[...] = jnp.zeros(l_ref.shape, F32)
        acc_ref[...] = jnp.zeros(acc_ref.shape, F32)

        def trip(g, carry):
            for u in range(TILES_PER_TRIP):
                tile(start(TILES_PER_TRIP * g + u), False)
            return carry

        lax.fori_loop(0, qi // TILES_PER_TRIP, trip, 0)
        base = (qi // TILES_PER_TRIP) * TILES_PER_TRIP
        rem = qi % TILES_PER_TRIP

        @pl.when(rem >= 2)
        def _():
            tile(start(base), False)
            tile(start(base + 1), False)

        last = base + jnp.where(rem >= 2, 2, 0)

        @pl.when(rem % 2 == 1)
        def _():
            tile(start(last), False)
            tile(start(qi), True)

        @pl.when(rem % 2 == 0)
        def _():
            tile(start(qi), True)

    def exact_row_max():
        for c in range(2):
            def fold(best, s):
                for u in range(n_lane_tiles):
                    best = jnp.maximum(best, s[:, u * LANES:(u + 1) * LANES])
                return best

            best = lax.fori_loop(0, qi, lambda t, best: fold(best, scores(start(t), c, False)),
                                 jnp.full((tq, LANES), MASK_VALUE, F32))
            best = fold(best, scores(start(qi), c, True))
            b_ref[c] = jnp.broadcast_to(jnp.max(best, axis=1, keepdims=True), (tq, LANES))

    for c in range(2):
        qq = q_ref[:, map_lanes(c)].astype(F32)
        b_ref[c] = jnp.sqrt(jnp.sum(qq * qq, axis=1, keepdims=True)) * kn_ref[c, 0:1, :]

    def one_pass(state):
        attempt, _ = state

        @pl.when(attempt == 1)
        def _():
            exact_row_max()

        sweep()
        l_min = jnp.min(jnp.minimum(jnp.sum(l_ref[0], axis=1, keepdims=True),
                                    jnp.sum(l_ref[1], axis=1, keepdims=True)))
        done = jnp.logical_or(l_min >= L_FLOOR, attempt == 1)
        return attempt + 1, done.astype(jnp.int32)

    lax.while_loop(lambda state: state[1] == 0, one_pass, (jnp.int32(0), jnp.int32(0)))

    lam = (jnp.exp(jnp.sum(lq1_ref[...] * lk1_ref[...], keepdims=True))
           - jnp.exp(jnp.sum(lq2_ref[...] * lk2_ref[...], keepdims=True)) + lambda_init)
    l0 = jnp.sum(l_ref[0], axis=1, keepdims=True)
    l1 = jnp.sum(l_ref[1], axis=1, keepdims=True)
    o = acc_ref[0] / l0 - lam * (acc_ref[1] / l1)
    o_ref[...] = (_rms_rows(o, sg_ref[...]) * (1.0 - lambda_init)).astype(o_ref.dtype)


def _diff_attention(qkv, lq1, lk1, lq2, lk2, subln_g, lambda_init, casts, tq):
    s = qkv.shape[0]
    vd = 2 * HEAD_DIM
    heads = qkv.shape[1] // (3 * vd)
    vec = pl.BlockSpec((1, HEAD_DIM), lambda h, i: (0, 0))
    return _host_call(
        functools.partial(_diff_attn_kernel, lambda_init=lambda_init), casts,
        grid=(heads, s // tq),
        in_specs=[
            pl.BlockSpec((tq, vd), lambda h, i: (i, h)),
            pl.BlockSpec((s, vd), lambda h, i: (0, heads + h)),
            pl.BlockSpec((s, vd), lambda h, i: (0, 2 * heads + h)),
            vec, vec, vec, vec,
            pl.BlockSpec((1, vd), lambda h, i: (0, 0)),
        ],
        out_spec=pl.BlockSpec((tq, vd), lambda h, i: (i, h)),
        out_shape=jax.ShapeDtypeStruct((s, heads * vd), BF16),
        scratch_shapes=[pltpu.VMEM((2, tq, LANES), F32), pltpu.VMEM((2, tq, LANES), F32),
                        pltpu.VMEM((2, tq, vd), F32), pltpu.VMEM((2, SUBLANES, LANES), F32)],
        name="diff_attention",
        args=(qkv, qkv, qkv, lq1, lk1, lq2, lk2, subln_g))


def _proj_residual_kernel(a_ref, h_ref, w_ref, pg_ref, o_ref):
    o_ref[...] = h_ref[...] + _rms_rows(_dot(a_ref[...], w_ref[...]), pg_ref[...])


def _proj_residual(a, h, w, post_g, tm):
    s, d = h.shape
    row = lambda i: (i, 0)
    return pl.pallas_call(
        _proj_residual_kernel,
        grid=(s // tm,),
        in_specs=[pl.BlockSpec((tm, a.shape[1]), row), pl.BlockSpec((tm, d), row),
                  _resident(w.shape), _resident((1, d))],
        out_specs=pl.BlockSpec((tm, d), row),
        out_shape=jax.ShapeDtypeStruct((s, d), F32),
        compiler_params=_params(1),
        name="proj_residual",
    )(a, h, w, post_g)


def _tiles(s, d, f):
    ffn_cols = 512 if f % 512 == 0 else 256
    return dict(
        proj_rows=min(1024, s), proj_cols=min(512, d), qkv_cols=min(1024, d),
        conv_rows=min(256, s),
        ffn_rows=min(1024, s), ffn_cols=min(ffn_cols, f),
        ple_rows=min(512, s),
        attn_rows=min(1024, s),
        out_rows=min(512, s),
    )


def kernel(x, p, pre_mix_g, post_mix_g, conv_w_pw1, conv_b_pw1, conv_w_dw, conv_b_dw, conv_ln_g, conv_ln_b, conv_w_pw2, conv_b_pw2, attn_w_qkv, attn_lambda_q1, attn_lambda_k1, attn_lambda_q2, attn_lambda_k2, attn_subln_g, attn_w_o, pre_ffn_g, post_ffn_g, ffn_w_in, ffn_w_dw, ffn_w_out, ple_pre_g, ple_w_gate, ple_b_gate, ple_w_proj, ple_post_g):
    batch, s, d = x.shape
    depth = p.shape[0]
    f = ffn_w_out.shape[1]
    t = _tiles(s, d, f)
    vec = lambda a: a.reshape(1, -1)
    p_all = p.reshape(depth * batch, s, p.shape[-1])
    first_proj = lambda i: (conv_w_pw1, i // 2) if i % 2 == 0 else (attn_w_qkv, i // 2)
    outs = []
    for b in range(batch):
        h = x.reshape(s, d) if batch == 1 else x[b]
        w_first, xn_next = first_proj(0)[0][0], None
        for i in range(depth):
            j = i // 2
            layer_casts = [(ffn_w_in, i), (ffn_w_out, i), (ple_w_gate, i), (ple_w_proj, i)]
            if i % 2 == 0:
                u, w_pw2_bf = _norm_glu(h, vec(pre_mix_g[i]), w_first, vec(conv_b_pw1[j]),
                                        [(conv_w_pw2, j)], t["proj_rows"], t["proj_cols"])
                h, w_in_bf, w_out_bf, w_gate_bf, w_proj_bf = _conv_mix(
                    u, h, conv_w_dw[j], vec(conv_b_dw[j]), vec(conv_ln_g[j]), vec(conv_ln_b[j]),
                    w_pw2_bf, vec(conv_b_pw2[j]), vec(post_mix_g[i]), layer_casts, t["conv_rows"])
            else:
                q_scale = HEAD_DIM ** -0.5 * math.log2(math.e)
                col_scale = jnp.concatenate([jnp.full((1, d), q_scale, F32),
                                             jnp.ones((1, 2 * d), F32)], axis=1)
                if xn_next is not None:
                    qkv, w_o_bf = _scaled_matmul(xn_next, w_first, col_scale, [(attn_w_o, j)],
                                                 t["proj_rows"], t["qkv_cols"], BF16)
                else:
                    qkv, w_o_bf = _norm_matmul(h, vec(pre_mix_g[i]), w_first, col_scale,
                                               [(attn_w_o, j)], t["proj_rows"], t["qkv_cols"], BF16)
                lambda_init = 0.8 - 0.6 * math.exp(-0.3 * i)
                o, w_in_bf, w_out_bf, w_gate_bf, w_proj_bf = _diff_attention(
                    qkv, vec(attn_lambda_q1[j]), vec(attn_lambda_k1[j]), vec(attn_lambda_q2[j]),
                    vec(attn_lambda_k2[j]), vec(attn_subln_g[j]), lambda_init,
                    layer_casts, t["attn_rows"])
                h = _proj_residual(o, h, w_o_bf, vec(post_mix_g[i]), t["out_rows"])
            h = _conv_ffn(h, vec(pre_ffn_g[i]), w_in_bf, ffn_w_dw, w_out_bf, i,
                          vec(post_ffn_g[i]), t["ffn_rows"], t["ffn_cols"])
            next_casts = [first_proj(i + 1)] if i + 1 < depth else []
            next_g = vec(pre_mix_g[i + 1]) if i + 1 < depth and (i + 1) % 2 == 1 else None
            h, *extra = _per_layer_embed(h, p_all, i * batch + b, vec(ple_pre_g[i]), w_gate_bf,
                                         vec(ple_b_gate[i]), w_proj_bf, vec(ple_post_g[i]),
                                         next_g, next_casts, t["ple_rows"])
            xn_next = extra.pop(0) if next_g is not None else None
            w_first = extra[0] if extra else None
        outs.append(h)
    return outs[0].reshape(1, s, d) if batch == 1 else jnp.stack(outs)
```

```python
import functools
import math

import jax
import jax.numpy as jnp
from jax import lax
from jax.experimental import pallas as pl
from jax.experimental.pallas import tpu as pltpu

F32 = jnp.float32
BF16 = jnp.bfloat16

RMS_EPS = 1e-6
LN_EPS = 1e-5
MASK_VALUE = -1e30
HEAD_DIM = 128
CONV_HALO = 32
CONV_LANE_CHUNKS = 4
SUBLANES = 8
LANES = 128
V7X_VMEM_LIMIT_BYTES = 56 * 1024 * 1024


def _params(n_axes, vmem_bytes=V7X_VMEM_LIMIT_BYTES):
    return pltpu.CompilerParams(dimension_semantics=("arbitrary",) * n_axes,
                                vmem_limit_bytes=vmem_bytes)


def _resident(shape):
    return pl.BlockSpec(shape, lambda *_: (0,) * len(shape), pipeline_mode=pl.Buffered(1))


def _cast_sidecar(w, layer, n_steps, step_of):
    rows, cols = w.shape[1:]
    bf16_rows = 2 * SUBLANES
    n_blocks = max(nb for nb in range(1, n_steps + 1)
                   if n_steps % nb == 0 and rows % nb == 0 and (rows // nb) % bf16_rows == 0)
    per, blk = n_steps // n_blocks, rows // n_blocks
    in_spec = pl.BlockSpec((None, blk, cols), lambda *g: (layer, step_of(*g) // per, 0))
    out_spec = pl.BlockSpec((blk, cols), lambda *g: (step_of(*g) // per, 0))
    return in_spec, out_spec, jax.ShapeDtypeStruct((rows, cols), BF16), per


def _host_call(body, casts, grid, in_specs, out_spec, out_shape, scratch_shapes, name, args):
    n_steps = math.prod(grid)

    def step_of(*g):
        step = 0
        for idx, extent in zip(g, grid):
            step = step * extent + idx
        return step

    sidecars = [_cast_sidecar(w, layer, n_steps, step_of) for w, layer in casts]
    out_specs = out_spec if isinstance(out_spec, tuple) else (out_spec,)
    out_shapes = out_shape if isinstance(out_shape, tuple) else (out_shape,)
    n_in, n_cast, n_out = len(in_specs), len(casts), len(out_specs)

    def kernel(*refs):
        ins, cast_ins = refs[:n_in], refs[n_in:n_in + n_cast]
        outs = refs[n_in + n_cast:n_in + n_cast + n_out]
        cast_outs = refs[n_in + n_cast + n_out:n_in + 2 * n_cast + n_out]
        scratch = refs[n_in + 2 * n_cast + n_out:]
        step = step_of(*(pl.program_id(a) for a in range(len(grid))))
        for (_, _, _, per), src, dst in zip(sidecars, cast_ins, cast_outs):
            @pl.when(step % per == 0)
            def _():
                dst[...] = src[...].astype(dst.dtype)
        body(*ins, *outs, *scratch)

    return pl.pallas_call(
        kernel,
        grid=grid,
        in_specs=list(in_specs) + [sc[0] for sc in sidecars],
        out_specs=out_specs + tuple(sc[1] for sc in sidecars),
        out_shape=out_shapes + tuple(sc[2] for sc in sidecars),
        scratch_shapes=scratch_shapes,
        compiler_params=_params(len(grid)),
        name=name,
    )(*args, *(w for w, _ in casts))


def _rms_rows(x, g):
    ms = jnp.mean(x * x, axis=-1, keepdims=True)
    return x * lax.rsqrt(ms + RMS_EPS) * g


def _sigmoid(x):
    return 1.0 / (1.0 + jnp.exp(-x))


def _dot(a, b):
    return jnp.dot(a, b, preferred_element_type=F32)


def _as_bf16(w):
    return w if w.dtype == BF16 else w.astype(BF16)


def _norm_to_scratch(dst_ref, src_ref, g, rows, dst_off=0, chunk=256):
    chunk = min(chunk, rows)

    def body(c, carry):
        r = pl.multiple_of(c * chunk, chunk)
        x = src_ref[pl.ds(r, chunk), :]
        dst_ref[pl.ds(dst_off + r, chunk), :] = _rms_rows(x, g).astype(dst_ref.dtype)
        return carry

    lax.fori_loop(0, rows // chunk, body, 0)


def _norm_glu_kernel(x_ref, g_ref, wa_ref, wg_ref, ba_ref, bg_ref, o_ref, xn_ref):
    @pl.when(pl.program_id(1) == 0)
    def _():
        _norm_to_scratch(xn_ref, x_ref, g_ref[...], x_ref.shape[0])

    xn = xn_ref[...]
    a = _dot(xn, _as_bf16(wa_ref[...])) + ba_ref[...]
    gate = _dot(xn, _as_bf16(wg_ref[...])) + bg_ref[...]
    o_ref[...] = (a * _sigmoid(gate)).astype(o_ref.dtype)


def _norm_glu(x, g, w, b, casts, tm, tn):
    s, d = x.shape
    n = w.shape[1] // 2
    nj = n // tn
    return _host_call(
        _norm_glu_kernel, casts,
        grid=(s // tm, nj),
        in_specs=[
            pl.BlockSpec((tm, d), lambda i, j: (i, 0)),
            pl.BlockSpec((1, d), lambda i, j: (0, 0)),
            pl.BlockSpec((d, tn), lambda i, j: (0, j)),
            pl.BlockSpec((d, tn), lambda i, j: (0, j + nj)),
            pl.BlockSpec((1, tn), lambda i, j: (0, j)),
            pl.BlockSpec((1, tn), lambda i, j: (0, j + nj)),
        ],
        out_spec=pl.BlockSpec((tm, tn), lambda i, j: (i, j)),
        out_shape=jax.ShapeDtypeStruct((s, n), F32),
        scratch_shapes=[pltpu.VMEM((tm, d), BF16)],
        name="norm_glu",
        args=(x, g, w, w, b, b))


def _norm_matmul_kernel(x_ref, g_ref, w_ref, cs_ref, o_ref, xn_ref):
    @pl.when(pl.program_id(1) == 0)
    def _():
        _norm_to_scratch(xn_ref, x_ref, g_ref[...], x_ref.shape[0])

    o_ref[...] = (_dot(xn_ref[...], _as_bf16(w_ref[...])) * cs_ref[...]).astype(o_ref.dtype)


def _scaled_matmul_kernel(x_ref, w_ref, cs_ref, o_ref):
    o_ref[...] = (_dot(x_ref[...], _as_bf16(w_ref[...])) * cs_ref[...]).astype(o_ref.dtype)


def _scaled_matmul(xn, w, col_scale, casts, tm, tn, out_dtype):
    s, d = xn.shape
    n = w.shape[1]
    return _host_call(
        _scaled_matmul_kernel, casts,
        grid=(s // tm, n // tn),
        in_specs=[
            pl.BlockSpec((tm, d), lambda i, j: (i, 0)),
            pl.BlockSpec((d, tn), lambda i, j: (0, j)),
            pl.BlockSpec((1, tn), lambda i, j: (0, j)),
        ],
        out_spec=pl.BlockSpec((tm, tn), lambda i, j: (i, j)),
        out_shape=jax.ShapeDtypeStruct((s, n), out_dtype),
        scratch_shapes=[],
        name="scaled_matmul",
        args=(xn, w, col_scale))


def _norm_matmul(x, g, w, col_scale, casts, tm, tn, out_dtype):
    s, d = x.shape
    n = w.shape[1]
    return _host_call(
        _norm_matmul_kernel, casts,
        grid=(s // tm, n // tn),
        in_specs=[
            pl.BlockSpec((tm, d), lambda i, j: (i, 0)),
            pl.BlockSpec((1, d), lambda i, j: (0, 0)),
            pl.BlockSpec((d, tn), lambda i, j: (0, j)),
            pl.BlockSpec((1, tn), lambda i, j: (0, j)),
        ],
        out_spec=pl.BlockSpec((tm, tn), lambda i, j: (i, j)),
        out_shape=jax.ShapeDtypeStruct((s, n), out_dtype),
        scratch_shapes=[pltpu.VMEM((tm, d), BF16)],
        name="norm_matmul",
        args=(x, g, w, col_scale))


def _dwconv_window(win, w8_ref, lanes, rows, k):
    first = CONV_HALO - (k - 1)
    n_win = rows + CONV_HALO
    acc = None
    for b in range(SUBLANES):
        offs = [o for o in range(b, CONV_HALO + 1, SUBLANES) if 0 <= o - first < k]
        if not offs:
            continue
        shifted = win if b == 0 else pltpu.roll(win, n_win - b, axis=0)
        for o in offs:
            tap = pltpu.repeat(w8_ref[o - first, :, lanes], rows // SUBLANES, axis=0)
            term = shifted[o - b:o - b + rows] * tap
            acc = term if acc is None else acc + term
    return acc


def _conv_mix_kernel(u_ref, halo_ref, h_ref, wdw_ref, bdw_ref, lng_ref, lnb_ref, w2_ref, b2_ref,
                     pg_ref, o_ref, ext_ref, act_ref, w8_ref, *, k, rows):
    tm, d = u_ref.shape
    ext_ref[0:CONV_HALO, :] = jnp.where(pl.program_id(0) > 0, halo_ref[...], 0.0)
    ext_ref[CONV_HALO:, :] = u_ref[...]
    for j in range(k):
        w8_ref[j] = jnp.broadcast_to(wdw_ref[j:j + 1, :], (SUBLANES, d))

    n_lane = d // LANES

    groups = n_lane // CONV_LANE_CHUNKS

    def conv_body(c, carry):
        r0 = pl.multiple_of((c // groups) * rows, rows)
        for chunk_in_group in range(CONV_LANE_CHUNKS):
            lane0 = ((c % groups) * CONV_LANE_CHUNKS + chunk_in_group) * LANES
            lanes = pl.ds(pl.multiple_of(lane0, LANES), LANES)
            win = ext_ref[pl.ds(r0, rows + CONV_HALO), lanes]
            y = _dwconv_window(win, w8_ref, lanes, rows, k) + bdw_ref[:, lanes]
            ext_ref[pl.ds(r0, rows), lanes] = y
        return carry

    lax.fori_loop(0, (tm // rows) * groups, conv_body, 0)

    chunk = min(256, tm)

    def ln_body(c, carry):
        r = pl.multiple_of(c * chunk, chunk)
        x = ext_ref[pl.ds(r, chunk), :]
        mu = jnp.mean(x, axis=-1, keepdims=True)
        xc = x - mu
        y = xc * lax.rsqrt(jnp.mean(xc * xc, axis=-1, keepdims=True) + LN_EPS)
        y = y * lng_ref[...] + lnb_ref[...]
        act_ref[pl.ds(r, chunk), :] = (y * _sigmoid(y)).astype(act_ref.dtype)
        return carry

    lax.fori_loop(0, tm // chunk, ln_body, 0)

    y = _dot(act_ref[...], w2_ref[...]) + b2_ref[...]
    o_ref[...] = h_ref[...] + _rms_rows(y, pg_ref[...])


def _conv_mix(u, h, w_dw, b_dw, ln_g, ln_b, w2, b2, post_g, casts, tm):
    s, d = u.shape
    k = w_dw.shape[0]
    assert d % (CONV_LANE_CHUNKS * LANES) == 0 and tm % CONV_HALO == 0
    halo_blocks = tm // CONV_HALO
    row = lambda i: (i, 0)
    return _host_call(
        functools.partial(_conv_mix_kernel, k=k, rows=min(128, tm)), casts,
        grid=(s // tm,),
        in_specs=[
            pl.BlockSpec((tm, d), row),
            pl.BlockSpec((CONV_HALO, d), lambda i: (jnp.maximum(i * halo_blocks - 1, 0), 0)),
            pl.BlockSpec((tm, d), row),
            _resident((k, d)),
            _resident((1, d)),
            _resident((1, d)),
            _resident((1, d)),
            _resident((d, d)),
            _resident((1, d)),
            _resident((1, d)),
        ],
        out_spec=pl.BlockSpec((tm, d), row),
        out_shape=jax.ShapeDtypeStruct((s, d), F32),
        scratch_shapes=[pltpu.VMEM((tm + CONV_HALO, d), F32), pltpu.VMEM((tm, d), BF16),
                        pltpu.VMEM((k, SUBLANES, d), F32)],
        name="conv_mix",
        args=(u, u, h, w_dw, b_dw, ln_g, ln_b, w2, b2, post_g))


def _causal3(u, tail, w):
    tm = u.shape[0]
    ext = jnp.concatenate([tail, u], axis=0)
    return (u * w[2:3]
            + ext[SUBLANES - 1:SUBLANES - 1 + tm] * w[1:2]
            + ext[SUBLANES - 2:SUBLANES - 2 + tm] * w[0:1])


def _ffn_kernel(h_ref, g_ref, wg_ref, wu_ref, dwg_ref, dwu_ref, wo_ref, pg_ref,
                o_ref, xn_ref, tail_ref):
    i = pl.program_id(0)
    j = pl.program_id(1)
    tm = h_ref.shape[0]

    @pl.when(j == 0)
    def _():
        _norm_to_scratch(xn_ref, h_ref, g_ref[...], tm)
        o_ref[...] = jnp.zeros(o_ref.shape, o_ref.dtype)

    @pl.when(i == 0)
    def _():
        tail_ref[j] = jnp.zeros(tail_ref.shape[1:], F32)

    xn = xn_ref[...]
    pre_gate = _dot(xn, wg_ref[...])
    pre_up = _dot(xn, wu_ref[...])
    gate = _causal3(pre_gate, tail_ref[j, 0], dwg_ref[...])
    up = _causal3(pre_up, tail_ref[j, 1], dwu_ref[...])
    tail_ref[j, 0] = pre_gate[tm - SUBLANES:]
    tail_ref[j, 1] = pre_up[tm - SUBLANES:]
    act = (gate * _sigmoid(gate) * up).astype(BF16)
    o_ref[...] += _dot(act, wo_ref[...])

    @pl.when(j == pl.num_programs(1) - 1)
    def _():
        chunk = min(256, tm)

        def body(c, carry):
            r = pl.multiple_of(c * chunk, chunk)
            rows = pl.ds(r, chunk)
            o_ref[rows, :] = h_ref[rows, :] + _rms_rows(o_ref[rows, :], pg_ref[...])
            return carry

        lax.fori_loop(0, tm // chunk, body, 0)


def _conv_ffn(h, pre_g, w_in, w_dw, w_out, layer, post_g, tm, tf):
    s, d = h.shape
    f = w_out.shape[0]
    nj = f // tf
    return pl.pallas_call(
        _ffn_kernel,
        grid=(s // tm, nj),
        in_specs=[
            pl.BlockSpec((tm, d), lambda i, j: (i, 0), pipeline_mode=pl.Buffered(1)),
            pl.BlockSpec((1, d), lambda i, j: (0, 0)),
            pl.BlockSpec((d, tf), lambda i, j: (0, j)),
            pl.BlockSpec((d, tf), lambda i, j: (0, j + nj)),
            pl.BlockSpec((None, 3, tf), lambda i, j: (layer, 0, j)),
            pl.BlockSpec((None, 3, tf), lambda i, j: (layer, 0, j + nj)),
            pl.BlockSpec((tf, d), lambda i, j: (j, 0)),
            pl.BlockSpec((1, d), lambda i, j: (0, 0)),
        ],
        out_specs=pl.BlockSpec((tm, d), lambda i, j: (i, 0)),
        out_shape=jax.ShapeDtypeStruct((s, d), F32),
        scratch_shapes=[pltpu.VMEM((tm, d), BF16), pltpu.VMEM((nj, 2, SUBLANES, tf), F32)],
        compiler_params=_params(2),
        name="conv_ffn",
    )(h, pre_g, w_in, w_in, w_dw, w_dw, w_out, post_g)


def _ple_kernel(h_ref, p_ref, g_ref, wg_ref, bg_ref, wp_ref, pg_ref, *rest):
    h = h_ref[...]
    xn = _rms_rows(h, g_ref[...]).astype(BF16)
    gate = _sigmoid(_dot(xn, wg_ref[...]) + bg_ref[...])
    emb = _dot(p_ref[...].astype(BF16), wp_ref[...])
    new_h = h + _rms_rows(gate * emb, pg_ref[...])
    if len(rest) == 1:
        rest[0][...] = new_h
    else:
        ng_ref, o_ref, xn_ref = rest
        o_ref[...] = new_h
        xn_ref[...] = _rms_rows(new_h, ng_ref[...]).astype(xn_ref.dtype)


def _per_layer_embed(h, p, p_index, pre_g, w_gate, b_gate, w_proj, post_g, next_g, casts, tm):
    s, d = h.shape
    e = p.shape[2]
    row = lambda i: (i, 0)
    block = pl.BlockSpec((tm, d), row)
    in_specs = [
        block,
        pl.BlockSpec((None, tm, e), lambda i: (p_index, i, 0)),
        _resident((1, d)),
        _resident((d, d)),
        _resident((1, d)),
        _resident((e, d)),
        _resident((1, d)),
    ]
    args = (h, p, pre_g, w_gate, b_gate, w_proj, post_g)
    out_spec, out_shape = block, jax.ShapeDtypeStruct((s, d), F32)
    if next_g is not None:
        in_specs, args = in_specs + [_resident((1, d))], args + (next_g,)
        out_spec = (block, block)
        out_shape = (out_shape, jax.ShapeDtypeStruct((s, d), BF16))
    return _host_call(_ple_kernel, casts, grid=(s // tm,), in_specs=in_specs, out_spec=out_spec,
                      out_shape=out_shape, scratch_shapes=[], name="per_layer_embed", args=args)


L_FLOOR = 2.0 ** -64
TILES_PER_TRIP = 4


def _diff_attn_kernel(q_ref, k_ref, v_ref, lq1_ref, lk1_ref, lq2_ref, lk2_ref, sg_ref, o_ref,
                      b_ref, l_ref, acc_ref, kn_ref, *, lambda_init):
    qi = pl.program_id(1)
    tq = q_ref.shape[0]
    tk = tq
    n_lane_tiles = tk // LANES
    n_keys = k_ref.shape[0]

    def map_lanes(c):
        return slice(c * HEAD_DIM, (c + 1) * HEAD_DIM)

    def scores(k0, c, diagonal):
        s = lax.dot_general(q_ref[:, map_lanes(c)], k_ref[pl.ds(k0, tk), map_lanes(c)],
                            (((1,), (1,)), ((), ())), preferred_element_type=F32)
        if diagonal:
            rows = lax.broadcasted_iota(jnp.int32, s.shape, 0)
            cols = lax.broadcasted_iota(jnp.int32, s.shape, 1)
            s = jnp.where(rows >= cols, s, MASK_VALUE)
        return s

    @pl.when(qi == 0)
    def _():
        chunk = min(512, n_keys)
        for c in range(2):
            def body(i, best):
                rows = pl.ds(pl.multiple_of(i * chunk, chunk), chunk)
                kk = k_ref[rows, map_lanes(c)].astype(F32)
                return jnp.maximum(best, jnp.sum(kk * kk, axis=1, keepdims=True))

            best = lax.fori_loop(0, n_keys // chunk, body, jnp.zeros((chunk, 1), F32))
            kn_ref[c] = jnp.broadcast_to(jnp.sqrt(jnp.max(best, axis=0, keepdims=True)),
                                         (SUBLANES, LANES))

    def block(row0, n_rows, k0, n_keys_blk, triangle):
        rows = slice(row0, row0 + n_rows)
        keys = pl.ds(k0, n_keys_blk)
        v = v_ref[keys, :]
        for c in range(2):
            s = lax.dot_general(q_ref[rows, map_lanes(c)], k_ref[keys, map_lanes(c)],
                                (((1,), (1,)), ((), ())), preferred_element_type=F32)
            if triangle:
                r_idx = lax.broadcasted_iota(jnp.int32, s.shape, 0)
                c_idx = lax.broadcasted_iota(jnp.int32, s.shape, 1)
                s = jnp.where(r_idx >= c_idx, s, MASK_VALUE)
            p = jnp.exp2(s - pltpu.repeat(b_ref[c, rows, :], n_keys_blk // LANES, axis=1))
            psum = p[:, 0:LANES]
            for t in range(1, n_keys_blk // LANES):
                psum = psum + p[:, t * LANES:(t + 1) * LANES]
            l_ref[c, rows, :] += psum
            acc_ref[c, rows, :] += _dot(p.astype(BF16), v)

    def tile(k0, diagonal):
        if not diagonal:
            block(0, tq, k0, tk, False)
        else:
            half = tk // 2
            block(0, tq, k0, half, True)
            block(half, tq - half, pl.multiple_of(k0 + half, half), half, True)

    def start(t):
        return pl.multiple_of(t * tk, tk)

    def sweep():
        l_ref[...] = jnp.zeros(l_ref.shape, F32)
        acc_ref[...] = jnp.zeros(acc_ref.shape, F32)

        def trip(g, carry):
            for u in range(TILES_PER_TRIP):
                tile(start(TILES_PER_TRIP * g + u), False)
            return carry

        lax.fori_loop(0, qi // TILES_PER_TRIP, trip, 0)
        base = (qi // TILES_PER_TRIP) * TILES_PER_TRIP
        rem = qi % TILES_PER_TRIP

        @pl.when(rem >= 2)
        def _():
            tile(start(base), False)
            tile(start(base + 1), False)

        last = base + jnp.where(rem >= 2, 2, 0)

        @pl.when(rem % 2 == 1)
        def _():
            tile(start(last), False)
            tile(start(qi), True)

        @pl.when(rem % 2 == 0)
        def _():
            tile(start(qi), True)

    def exact_row_max():
        for c in range(2):
            def fold(best, s):
                for u in range(n_lane_tiles):
                    best = jnp.maximum(best, s[:, u * LANES:(u + 1) * LANES])
                return best

            best = lax.fori_loop(0, qi, lambda t, best: fold(best, scores(start(t), c, False)),
                                 jnp.full((tq, LANES), MASK_VALUE, F32))
            best = fold(best, scores(start(qi), c, True))
            b_ref[c] = jnp.broadcast_to(jnp.max(best, axis=1, keepdims=True), (tq, LANES))

    for c in range(2):
        qq = q_ref[:, map_lanes(c)].astype(F32)
        b_ref[c] = jnp.sqrt(jnp.sum(qq * qq, axis=1, keepdims=True)) * kn_ref[c, 0:1, :]

    def one_pass(state):
        attempt, _ = state

        @pl.when(attempt == 1)
        def _():
            exact_row_max()

        sweep()
        l_min = jnp.min(jnp.minimum(jnp.sum(l_ref[0], axis=1, keepdims=True),
                                    jnp.sum(l_ref[1], axis=1, keepdims=True)))
        done = jnp.logical_or(l_min >= L_FLOOR, attempt == 1)
        return attempt + 1, done.astype(jnp.int32)

    lax.while_loop(lambda state: state[1] == 0, one_pass, (jnp.int32(0), jnp.int32(0)))

    lam = (jnp.exp(jnp.sum(lq1_ref[...] * lk1_ref[...], keepdims=True))
           - jnp.exp(jnp.sum(lq2_ref[...] * lk2_ref[...], keepdims=True)) + lambda_init)
    l0 = jnp.sum(l_ref[0], axis=1, keepdims=True)
    l1 = jnp.sum(l_ref[1], axis=1, keepdims=True)
    o = acc_ref[0] / l0 - lam * (acc_ref[1] / l1)
    o_ref[...] = (_rms_rows(o, sg_ref[...]) * (1.0 - lambda_init)).astype(o_ref.dtype)


def _diff_attention(qkv, lq1, lk1, lq2, lk2, subln_g, lambda_init, casts, tq):
    s = qkv.shape[0]
    vd = 2 * HEAD_DIM
    heads = qkv.shape[1] // (3 * vd)
    vec = pl.BlockSpec((1, HEAD_DIM), lambda h, i: (0, 0))
    return _host_call(
        functools.partial(_diff_attn_kernel, lambda_init=lambda_init), casts,
        grid=(heads, s // tq),
        in_specs=[
            pl.BlockSpec((tq, vd), lambda h, i: (i, h)),
            pl.BlockSpec((s, vd), lambda h, i: (0, heads + h)),
            pl.BlockSpec((s, vd), lambda h, i: (0, 2 * heads + h)),
            vec, vec, vec, vec,
            pl.BlockSpec((1, vd), lambda h, i: (0, 0)),
        ],
        out_spec=pl.BlockSpec((tq, vd), lambda h, i: (i, h)),
        out_shape=jax.ShapeDtypeStruct((s, heads * vd), BF16),
        scratch_shapes=[pltpu.VMEM((2, tq, LANES), F32), pltpu.VMEM((2, tq, LANES), F32),
                        pltpu.VMEM((2, tq, vd), F32), pltpu.VMEM((2, SUBLANES, LANES), F32)],
        name="diff_attention",
        args=(qkv, qkv, qkv, lq1, lk1, lq2, lk2, subln_g))


def _proj_residual_kernel(a_ref, h_ref, w_ref, pg_ref, o_ref):
    o_ref[...] = h_ref[...] + _rms_rows(_dot(a_ref[...], w_ref[...]), pg_ref[...])


def _proj_residual(a, h, w, post_g, tm):
    s, d = h.shape
    row = lambda i: (i, 0)
    return pl.pallas_call(
        _proj_residual_kernel,
        grid=(s // tm,),
        in_specs=[pl.BlockSpec((tm, a.shape[1]), row), pl.BlockSpec((tm, d), row),
                  _resident(w.shape), _resident((1, d))],
        out_specs=pl.BlockSpec((tm, d), row),
        out_shape=jax.ShapeDtypeStruct((s, d), F32),
        compiler_params=_params(1),
        name="proj_residual",
    )(a, h, w, post_g)


def _tiles(s, d, f):
    ffn_cols = 512 if f % 512 == 0 else 256
    return dict(
        proj_rows=min(1024, s), proj_cols=min(512, d), qkv_cols=min(1024, d), qkv_wide_cols=min(2048, d),
        conv_rows=min(256, s),
        ffn_rows=min(1024, s), ffn_cols=min(ffn_cols, f),
        ple_rows=min(512, s),
        attn_rows=min(1024, s),
        out_rows=min(512, s),
    )


def kernel(x, p, pre_mix_g, post_mix_g, conv_w_pw1, conv_b_pw1, conv_w_dw, conv_b_dw, conv_ln_g, conv_ln_b, conv_w_pw2, conv_b_pw2, attn_w_qkv, attn_lambda_q1, attn_lambda_k1, attn_lambda_q2, attn_lambda_k2, attn_subln_g, attn_w_o, pre_ffn_g, post_ffn_g, ffn_w_in, ffn_w_dw, ffn_w_out, ple_pre_g, ple_w_gate, ple_b_gate, ple_w_proj, ple_post_g):
    batch, s, d = x.shape
    depth = p.shape[0]
    f = ffn_w_out.shape[1]
    t = _tiles(s, d, f)
    vec = lambda a: a.reshape(1, -1)
    p_all = p.reshape(depth * batch, s, p.shape[-1])
    first_proj = lambda i: (conv_w_pw1, i // 2) if i % 2 == 0 else (attn_w_qkv, i // 2)
    outs = []
    for b in range(batch):
        h = x.reshape(s, d) if batch == 1 else x[b]
        w_first, xn_next = first_proj(0)[0][0], None
        for i in range(depth):
            j = i // 2
            layer_casts = [(ffn_w_in, i), (ffn_w_out, i), (ple_w_gate, i), (ple_w_proj, i)]
            if i % 2 == 0:
                u, w_pw2_bf = _norm_glu(h, vec(pre_mix_g[i]), w_first, vec(conv_b_pw1[j]),
                                        [(conv_w_pw2, j)], t["proj_rows"], t["proj_cols"])
                h, w_in_bf, w_out_bf, w_gate_bf, w_proj_bf = _conv_mix(
                    u, h, conv_w_dw[j], vec(conv_b_dw[j]), vec(conv_ln_g[j]), vec(conv_ln_b[j]),
                    w_pw2_bf, vec(conv_b_pw2[j]), vec(post_mix_g[i]), layer_casts, t["conv_rows"])
            else:
                q_scale = HEAD_DIM ** -0.5 * math.log2(math.e)
                col_scale = jnp.concatenate([jnp.full((1, d), q_scale, F32),
                                             jnp.ones((1, 2 * d), F32)], axis=1)
                if xn_next is not None:
                    qkv, w_o_bf = _scaled_matmul(xn_next, w_first, col_scale, [(attn_w_o, j)],
                                                 t["proj_rows"], t["qkv_wide_cols"], BF16)
                else:
                    qkv, w_o_bf = _norm_matmul(h, vec(pre_mix_g[i]), w_first, col_scale,
                                               [(attn_w_o, j)], t["proj_rows"], t["qkv_cols"], BF16)
                lambda_init = 0.8 - 0.6 * math.exp(-0.3 * i)
                o, w_in_bf, w_out_bf, w_gate_bf, w_proj_bf = _diff_attention(
                    qkv, vec(attn_lambda_q1[j]), vec(attn_lambda_k1[j]), vec(attn_lambda_q2[j]),
                    vec(attn_lambda_k2[j]), vec(attn_subln_g[j]), lambda_init,
                    layer_casts, t["attn_rows"])
                h = _proj_residual(o, h, w_o_bf, vec(post_mix_g[i]), t["out_rows"])
            h = _conv_ffn(h, vec(pre_ffn_g[i]), w_in_bf, ffn_w_dw, w_out_bf, i,
                          vec(post_ffn_g[i]), t["ffn_rows"], t["ffn_cols"])
            next_casts = [first_proj(i + 1)] if i + 1 < depth else []
            next_g = vec(pre_mix_g[i + 1]) if i + 1 < depth and (i + 1) % 2 == 1 else None
            h, *extra = _per_layer_embed(h, p_all, i * batch + b, vec(ple_pre_g[i]), w_gate_bf,
                                         vec(ple_b_gate[i]), w_proj_bf, vec(ple_post_g[i]),
                                         next_g, next_casts, t["ple_rows"])
            xn_next = extra.pop(0) if next_g is not None else None
            w_first = extra[0] if extra else None
        outs.append(h)
    return outs[0].reshape(1, s, d) if batch == 1 else jnp.stack(outs)
```
